```python
import math
import numpy as np
import jax
import jax.numpy as jnp
from jax import lax

D_MODEL = 1024
BATCH = 4
SEQ = 4096
DEPTH = 2
DEC_BATCH = 32
DEC_SEQ = 1
PAST_LEN = 16384
PAGE_SIZE = 128

HEAD_DIM = 64
ROPE_DIM = HEAD_DIM // 4
ROPE_THETA = 500000.0
Q_BLOCK = 128
A_HEADS = D_MODEL // HEAD_DIM // 2
A_KV_HEADS = 2
IDX_HEADS = 8
IDX_DIM = 64
IDX_TOPK = 256
B_HEADS = D_MODEL // HEAD_DIM // 2
B_KV_HEADS = 2
CMP_LEN = 32
CMP_STRIDE = 16
CMP_HID = 128
SLC_LEN = 64
SLC_TOPN = 16
WINDOW = 512
FORCE_BONUS = 100.0
C_HEADS = D_MODEL // HEAD_DIM
D_FF = 2816
N_EXPERTS = 8
TOP_K = 2
D_FF_EXPERT = 3584
N_EVEN = (DEPTH + 1) // 2
N_ODD = DEPTH // 2
ALPHA = (2 * DEPTH) ** 0.25
BETA = (8 * DEPTH) ** -0.25
LN_EPS = 1e-5
NEG = -1e30
POOL_EXTRA = 4
EVEN_SPLIT = (A_HEADS * HEAD_DIM, A_KV_HEADS * HEAD_DIM, A_KV_HEADS * HEAD_DIM,
              IDX_HEADS * IDX_DIM, IDX_DIM, IDX_HEADS,
              B_HEADS * HEAD_DIM, 6 * B_KV_HEADS * HEAD_DIM, 3 * B_HEADS)
EVEN_IN = sum(EVEN_SPLIT)
EVEN_OFFSETS = tuple(int(v) for v in np.cumsum(EVEN_SPLIT)[:-1])
ODD_IN = 3 * C_HEADS * HEAD_DIM

kernel_name = 'hybrid_dsa_nsa_stickbreak_decoder_step'


def layer_norm(x, g, b):
    xf = x.astype(jnp.float32)
    mu = jnp.mean(xf, -1, keepdims=True)
    var = jnp.mean(jnp.square(xf - mu), -1, keepdims=True)
    return ((xf - mu) * lax.rsqrt(var + LN_EPS) * g + b).astype(x.dtype)


def ada(c, w, b):
    m = jax.nn.silu(c) @ w + b
    shift, scale, gate = jnp.split(m[:, None, :], 3, axis=-1)
    return shift, scale, gate


def rope(x, pos):
    half = ROPE_DIM // 2
    inv = ROPE_THETA ** (-jnp.arange(half, dtype=jnp.float32) / half)
    ang = pos.astype(jnp.float32)[:, None] * inv[None, :]
    cos, sin = jnp.cos(ang)[:, None, :], jnp.sin(ang)[:, None, :]
    x1 = x[..., :half].astype(jnp.float32)
    x2 = x[..., half:ROPE_DIM].astype(jnp.float32)
    rot = jnp.concatenate([x1 * cos - x2 * sin, x1 * sin + x2 * cos], -1).astype(x.dtype)
    return jnp.concatenate([rot, x[..., ROPE_DIM:]], -1)


def masked_softmax(s, mask):
    s = jnp.where(mask, s.astype(jnp.float32), NEG)
    return jax.nn.softmax(s, axis=-1) * mask


def shared_attend(q, k, v, mask):
    N, Tq, H, d = q.shape
    G = k.shape[2]
    qg = q.reshape(N, Tq, G, H // G, d)
    s = jnp.einsum('ntgrd,nsgd->ntgrs', qg, k) / math.sqrt(d)
    p = masked_softmax(s, mask[None, :, None, None, :])
    o = jnp.einsum('ntgrs,nsgd->ntgrd', p.astype(v.dtype), v)
    return o.reshape(N, Tq, H, d), p


def gathered_attend(q, k, v, valid):
    N, Tq, H, d = q.shape
    G = k.shape[2]
    qg = q.reshape(N, Tq, G, H // G, d)
    s = jnp.einsum('ntgrd,ntgsd->ntgrs', qg, k) / math.sqrt(d)
    p = masked_softmax(s, valid[:, :, :, None, :])
    o = jnp.einsum('ntgrs,ntgsd->ntgrd', p.astype(v.dtype), v)
    return o.reshape(N, Tq, H, d)


def map_query_blocks(fn, T):
    out = lax.map(fn, jnp.arange(T // Q_BLOCK))
    out = jnp.moveaxis(out, 0, 1)
    return out.reshape((out.shape[0], T) + out.shape[3:])


def gather_pages(pool, layer, page_table):
    g = pool[layer, page_table]
    return g.reshape((g.shape[0], g.shape[1] * g.shape[2]) + g.shape[3:])


def even_project(h, pos, w_in):
    N, T, _ = h.shape
    qa, ka, va, qi, ki, wi, qb, kvb, gb = jnp.split(h @ w_in, EVEN_OFFSETS, axis=-1)
    qa = rope(qa.reshape(N, T, A_HEADS, HEAD_DIM), pos)
    ka = rope(ka.reshape(N, T, A_KV_HEADS, HEAD_DIM), pos)
    va = va.reshape(N, T, A_KV_HEADS, HEAD_DIM)
    qi = rope(qi.reshape(N, T, IDX_HEADS, IDX_DIM), pos)
    ki = rope(ki.reshape(N, T, 1, IDX_DIM), pos)[:, :, 0]
    qb = qb.reshape(N, T, B_HEADS, HEAD_DIM)
    kvb = kvb.reshape(N, T, 6, B_KV_HEADS, HEAD_DIM)
    kc, vc, vs, vw = kvb[:, :, 0], kvb[:, :, 1], kvb[:, :, 3], kvb[:, :, 5]
    ks = rope(kvb[:, :, 2], pos)
    kw = rope(kvb[:, :, 4], pos)
    gb = gb.reshape(N, T, B_HEADS, 3)
    return (qa, ka, va, qi, ki, wi), (qb, gb, kc, vc, ks, vs, kw, vw)


def odd_project(h, w_in):
    N, T, _ = h.shape
    q, k, v = jnp.split(h @ w_in, 3, axis=-1)
    shp = (N, T, C_HEADS, HEAD_DIM)
    return q.reshape(shp), k.reshape(shp), v.reshape(shp)


def combine_heads(oa, ob):
    N, T = oa.shape[:2]
    return jnp.concatenate([oa.reshape(N, T, -1), ob.reshape(N, T, -1)], -1)


def indexer_topk(qi, wi, kidx, q_pos, topk):
    s = jax.nn.relu(jnp.einsum('nthd,nsd->nths', qi, kidx))
    score = jnp.einsum('nth,nths->nts', wi, s).astype(jnp.float32)
    k_pos = jnp.arange(kidx.shape[1])
    causal = k_pos[None, :] <= q_pos[:, None]
    score = jnp.where(causal[None], score, NEG)
    _, idx = lax.top_k(score, topk)
    return idx, idx <= q_pos[None, :, None]


def dsa_prompt(qa, qi, wi, ka, va, ki, pos):
    N, T = qa.shape[:2]
    topk = min(IDX_TOPK, T // 4)
    bn = jnp.arange(N)[:, None, None]

    def blk(i):
        s0 = i * Q_BLOCK
        sl = lambda a: lax.dynamic_slice_in_dim(a, s0, Q_BLOCK, axis=1)
        qpos = lax.dynamic_slice_in_dim(pos, s0, Q_BLOCK)
        idx, valid = indexer_topk(sl(qi), sl(wi), ki, qpos, topk)
        kg = ka[bn, idx].transpose(0, 1, 3, 2, 4)
        vg = va[bn, idx].transpose(0, 1, 3, 2, 4)
        return gathered_attend(sl(qa), kg, vg, valid[:, :, None, :])

    return map_query_blocks(blk, T)


def dsa_sample(qa, qi, wi, ka, va, ki, cache_k, cache_v, cache_kidx, layer, page_table, pos):
    N, Tq = qa.shape[:2]
    past = page_table.shape[1] * PAGE_SIZE
    kidx_all = jnp.concatenate([gather_pages(cache_kidx, layer, page_table), ki], axis=1)
    topk = min(IDX_TOPK, kidx_all.shape[1] // 4)
    idx, valid = indexer_topk(qi, wi, kidx_all, pos, topk)
    bn = jnp.arange(N)[:, None, None]
    pidx = jnp.minimum(idx, past - 1)
    phys = page_table[bn, pidx // PAGE_SIZE]
    off = pidx % PAGE_SIZE
    nidx = jnp.clip(idx - past, 0, Tq - 1)
    is_past = (idx < past)[..., None, None]
    kg = jnp.where(is_past, cache_k[layer, phys, off], ka[bn, nidx]).transpose(0, 1, 3, 2, 4)
    vg = jnp.where(is_past, cache_v[layer, phys, off], va[bn, nidx]).transpose(0, 1, 3, 2, 4)
    return gathered_attend(qa, kg, vg, valid[:, :, None, :])


def compress(rows, pos_emb, w1, w2):
    N, L, G, d = rows.shape
    n_cmp = (L - CMP_LEN) // CMP_STRIDE + 1
    r = CMP_LEN // CMP_STRIDE
    chunks = rows[:, :(n_cmp + r - 1) * CMP_STRIDE].reshape(N, n_cmp + r - 1, CMP_STRIDE, G, d)
    w1c = w1.reshape(r, CMP_STRIDE, d, CMP_HID)
    h = pos_emb.reshape(-1) @ w1
    for j in range(r):
        h = h + jnp.einsum('nclgd,ldh->ncgh', chunks[:, j:j + n_cmp], w1c[j])
    return jax.nn.silu(h) @ w2


def to_blocks(rows):
    N, L, G, d = rows.shape
    n_slc = -(-L // SLC_LEN)
    rows = jnp.pad(rows, ((0, 0), (0, n_slc * SLC_LEN - L), (0, 0), (0, 0)))
    return rows.reshape(N, n_slc, SLC_LEN, G, d)


def cmp_to_slc(n_cmp, n_slc):
    i = np.arange(n_cmp)[:, None] * CMP_STRIDE
    j = np.arange(n_slc)[None, :] * SLC_LEN
    return jnp.asarray((i < j + SLC_LEN) & (i + CMP_LEN > j), dtype=jnp.float32)


def nsa_attend(q_rot, q_raw, gates, q_pos, kc, vc, ks_blk, vs_blk, kw, vw, kw_pos):
    N, Tq, H, d = q_rot.shape
    G = kc.shape[2]
    n_cmp, n_slc = kc.shape[1], ks_blk.shape[1]
    cmp_end = jnp.arange(n_cmp) * CMP_STRIDE + (CMP_LEN - 1)
    o_cmp, p_cmp = shared_attend(q_raw, kc, vc, cmp_end[None, :] <= q_pos[:, None])
    p_slc = jnp.einsum('ntgrc,cj->ntgj', p_cmp, cmp_to_slc(n_cmp, n_slc))
    blk = jnp.arange(n_slc)[None, :]
    cur = (q_pos // SLC_LEN)[:, None]
    forced = (blk == 0) | (blk == cur) | (blk == cur - 1)
    avail = blk * SLC_LEN <= q_pos[:, None]
    score = jnp.where(avail[None, :, None, :], p_slc + FORCE_BONUS * forced[None, :, None, :], NEG)
    _, sel = lax.top_k(score, min(SLC_TOPN, n_slc))
    bn = jnp.arange(N)[:, None, None, None]
    gi = jnp.arange(G)[None, None, :, None]
    ks = ks_blk[bn, sel, :, gi].reshape(N, Tq, G, -1, d)
    vs = vs_blk[bn, sel, :, gi].reshape(N, Tq, G, -1, d)
    sel_pos = (sel[..., None] * SLC_LEN + jnp.arange(SLC_LEN)).reshape(N, Tq, G, -1)
    o_slc = gathered_attend(q_rot, ks, vs, sel_pos <= q_pos[None, :, None, None])
    wmask = ((kw_pos[None, :] <= q_pos[:, None]) & (kw_pos[None, :] > q_pos[:, None] - WINDOW)
             & (kw_pos[None, :] >= 0))
    o_win, _ = shared_attend(q_rot, kw, vw, wmask)
    g = jax.nn.sigmoid(gates.astype(jnp.float32)).astype(q_rot.dtype)
    return g[..., 0:1] * o_cmp + g[..., 1:2] * o_slc + g[..., 2:3] * o_win


def nsa_prompt(qb, gb, kc, vc, ks, vs, kw, vw, pos, cmp_k, cmp_v):
    N, T = qb.shape[:2]
    q_rot = rope(qb, pos)
    kcc, vcc = compress(kc, *cmp_k), compress(vc, *cmp_v)
    ks_blk, vs_blk = to_blocks(ks), to_blocks(vs)
    pad = ((0, 0), (WINDOW, 0), (0, 0), (0, 0))
    kw_pad, vw_pad = jnp.pad(kw, pad), jnp.pad(vw, pad)

    def blk(i):
        s0 = i * Q_BLOCK
        sl = lambda a: lax.dynamic_slice_in_dim(a, s0, Q_BLOCK, axis=1)
        band = lambda a: lax.dynamic_slice_in_dim(a, s0, WINDOW + Q_BLOCK, axis=1)
        qpos = lax.dynamic_slice_in_dim(pos, s0, Q_BLOCK)
        kw_pos = s0 - WINDOW + jnp.arange(WINDOW + Q_BLOCK)
        return nsa_attend(sl(q_rot), sl(qb), sl(gb), qpos, kcc, vcc, ks_blk, vs_blk,
                          band(kw_pad), band(vw_pad), kw_pos)

    return map_query_blocks(blk, T)


def nsa_sample(qb, gb, kc, vc, ks, vs, kw, vw, pos, cache_ck, cache_cv, cache_sk, cache_sv,
               win_k, win_v, layer, page_table, cmp_k, cmp_v):
    cat = lambda pool, new: jnp.concatenate([gather_pages(pool, layer, page_table), new], axis=1)
    kcc = compress(cat(cache_ck, kc), *cmp_k)
    vcc = compress(cat(cache_cv, vc), *cmp_v)
    ks_blk, vs_blk = to_blocks(cat(cache_sk, ks)), to_blocks(cat(cache_sv, vs))
    W, Tq = win_k.shape[1], qb.shape[1]
    kw_all = jnp.concatenate([win_k, kw], axis=1)
    vw_all = jnp.concatenate([win_v, vw], axis=1)
    kw_pos = pos[0] - W + jnp.arange(W + Tq)
    o = nsa_attend(rope(qb, pos), qb, gb, pos, kcc, vcc, ks_blk, vs_blk, kw_all, vw_all, kw_pos)
    return o, kw_all[:, -W:], vw_all[:, -W:]


def sb_block(q, k, v, mask, log_carry):
    z = jnp.einsum('nthd,nshd->nhts', q, k).astype(jnp.float32) / math.sqrt(q.shape[-1])
    u = jnp.where(mask, jax.nn.log_sigmoid(-z), 0.0)
    later = lax.cumsum(u, axis=3, reverse=True) - u + log_carry[..., None]
    a = jnp.where(mask, jnp.exp(jax.nn.log_sigmoid(z) + later), 0.0)
    o = jnp.einsum('nhts,nshd->nthd', a.astype(v.dtype), v)
    return o, log_carry + u.sum(-1)


def sb_prompt(q, k, v, pos):
    N, T, H, _ = q.shape

    def blk(i):
        s0 = i * Q_BLOCK
        qpos = lax.dynamic_slice_in_dim(pos, s0, Q_BLOCK)
        o, _ = sb_block(lax.dynamic_slice_in_dim(q, s0, Q_BLOCK, axis=1), k, v,
                        pos[None, :] < qpos[:, None], jnp.zeros((N, H, Q_BLOCK), jnp.float32))
        return o

    return map_query_blocks(blk, T)


def sb_sample(q, k_new, v_new, cache_k, cache_v, layer, page_table, pos):
    N, Tq, H, _ = q.shape
    o, lc = sb_block(q, k_new, v_new, pos[None, :] < pos[:, None], jnp.zeros((N, H, Tq), jnp.float32))

    def step(carry, phys):
        o_acc, lcar = carry
        ob, lcar = sb_block(q, cache_k[layer, phys], cache_v[layer, phys],
                            jnp.ones((Tq, PAGE_SIZE), bool), lcar)
        return (o_acc + ob, lcar), None

    (o, _), _ = lax.scan(step, (o, lc), page_table.T[::-1])
    return o


def swiglu(h, wg, wu, wd):
    return (jax.nn.silu(h @ wg) * (h @ wu)) @ wd


def moe_ffn(h, w_router, b_router, wg, wu, wd):
    logits = (h @ w_router + b_router).astype(jnp.float32)
    top_val, top_idx = lax.top_k(logits, TOP_K)
    gate = jax.nn.softmax(top_val, axis=-1)
    dense_gate = jnp.einsum('ntk,ntke->nte', gate,
                            jax.nn.one_hot(top_idx, N_EXPERTS, dtype=jnp.float32)).astype(h.dtype)
    y = jnp.zeros_like(h)
    for e in range(N_EXPERTS):
        y = y + dense_gate[..., e:e + 1] * swiglu(h, wg[e], wu[e], wd[e])
    return y


def setup_inputs(seed: int = 0) -> dict:
    key = jax.random.key(seed)
    keys = iter(jax.random.split(key, 64))

    def nrm(shape, scale):
        return jax.random.normal(next(keys), shape, jnp.float32) * scale

    D = D_MODEL
    n_pages = PAST_LEN // PAGE_SIZE
    n_used = DEC_BATCH * n_pages
    n_pool = n_used + n_used // POOL_EXTRA
    win = min(WINDOW, PAST_LEN)
    pool_a = (N_EVEN, n_pool, PAGE_SIZE, A_KV_HEADS, HEAD_DIM)
    pool_b = (N_EVEN, n_pool, PAGE_SIZE, B_KV_HEADS, HEAD_DIM)
    pool_c = (N_ODD, n_pool, PAGE_SIZE, C_HEADS, HEAD_DIM)
    x_prompt = nrm((BATCH, SEQ, D), 1.0)
    x_sample = nrm((DEC_BATCH, DEC_SEQ, D), 1.0)
    cache_a_k = nrm(pool_a, 1.0)
    cache_a_v = nrm(pool_a, 1.0)
    cache_a_kidx = nrm((N_EVEN, n_pool, PAGE_SIZE, IDX_DIM), 1.0)
    cache_b_cmp_k = nrm(pool_b, 1.0)
    cache_b_cmp_v = nrm(pool_b, 1.0)
    cache_b_slc_k = nrm(pool_b, 1.0)
    cache_b_slc_v = nrm(pool_b, 1.0)
    state_b_win_k = nrm((N_EVEN, DEC_BATCH, win, B_KV_HEADS, HEAD_DIM), 1.0)
    state_b_win_v = nrm((N_EVEN, DEC_BATCH, win, B_KV_HEADS, HEAD_DIM), 1.0)
    cache_c_k = nrm(pool_c, 1.0)
    cache_c_v = nrm(pool_c, 1.0)
    page_table = jax.random.permutation(next(keys), n_pool)[:n_used].reshape(
        DEC_BATCH, n_pages).astype(jnp.int32)
    return {
        'x_prompt': x_prompt, 'x_sample': x_sample,
        'cache_a_k': cache_a_k, 'cache_a_v': cache_a_v, 'cache_a_kidx': cache_a_kidx,
        'cache_b_cmp_k': cache_b_cmp_k, 'cache_b_cmp_v': cache_b_cmp_v,
        'cache_b_slc_k': cache_b_slc_k, 'cache_b_slc_v': cache_b_slc_v,
        'state_b_win_k': state_b_win_k, 'state_b_win_v': state_b_win_v,
        'cache_c_k': cache_c_k, 'cache_c_v': cache_c_v,
        'page_table': page_table,
        'c_prompt': nrm((BATCH, D), 1.0),
        'c_sample': nrm((DEC_BATCH, D), 1.0),
        'w_ada_mix': nrm((DEPTH, D, 3 * D), 0.5 * D ** -0.5),
        'b_ada_mix': nrm((DEPTH, 3 * D), 0.02),
        'ln_mix_g': 1.0 + nrm((DEPTH, D), 0.02),
        'ln_mix_b': nrm((DEPTH, D), 0.02),
        'w_ada_ffn': nrm((DEPTH, D, 3 * D), 0.5 * D ** -0.5),
        'b_ada_ffn': nrm((DEPTH, 3 * D), 0.02),
        'ln_ffn_g': 1.0 + nrm((DEPTH, D), 0.02),
        'ln_ffn_b': nrm((DEPTH, D), 0.02),
        'w_in_even': nrm((N_EVEN, D, EVEN_IN), D ** -0.5),
        'w_out_even': nrm((N_EVEN, (A_HEADS + B_HEADS) * HEAD_DIM, D),
                          BETA * ((A_HEADS + B_HEADS) * HEAD_DIM) ** -0.5),
        'cmp_pos_k': nrm((N_EVEN, CMP_LEN, HEAD_DIM), 0.1),
        'cmp_w1_k': nrm((N_EVEN, CMP_LEN * HEAD_DIM, CMP_HID), (CMP_LEN * HEAD_DIM) ** -0.5),
        'cmp_w2_k': nrm((N_EVEN, CMP_HID, HEAD_DIM), CMP_HID ** -0.5),
        'cmp_pos_v': nrm((N_EVEN, CMP_LEN, HEAD_DIM), 0.1),
        'cmp_w1_v': nrm((N_EVEN, CMP_LEN * HEAD_DIM, CMP_HID), (CMP_LEN * HEAD_DIM) ** -0.5),
        'cmp_w2_v': nrm((N_EVEN, CMP_HID, HEAD_DIM), CMP_HID ** -0.5),
        'w_ffn_gate': nrm((N_EVEN, D, D_FF), D ** -0.5),
        'w_ffn_up': nrm((N_EVEN, D, D_FF), D ** -0.5),
        'w_ffn_down': nrm((N_EVEN, D_FF, D), BETA * D_FF ** -0.5),
        'w_in_odd': nrm((N_ODD, D, ODD_IN), D ** -0.5),
        'w_out_odd': nrm((N_ODD, C_HEADS * HEAD_DIM, D), BETA * (C_HEADS * HEAD_DIM) ** -0.5),
        'w_router': nrm((N_ODD, D, N_EXPERTS), D ** -0.5),
        'b_router': nrm((N_ODD, N_EXPERTS), 0.01),
        'w_moe_gate': nrm((N_ODD, N_EXPERTS, D, D_FF_EXPERT), D ** -0.5),
        'w_moe_up': nrm((N_ODD, N_EXPERTS, D, D_FF_EXPERT), D ** -0.5),
        'w_moe_down': nrm((N_ODD, N_EXPERTS, D_FF_EXPERT, D), BETA * D_FF_EXPERT ** -0.5),
    }


def reference(x_prompt, x_sample, cache_a_k, cache_a_v, cache_a_kidx, cache_b_cmp_k, cache_b_cmp_v,
              cache_b_slc_k, cache_b_slc_v, state_b_win_k, state_b_win_v, cache_c_k, cache_c_v,
              page_table, c_prompt, c_sample, w_ada_mix, b_ada_mix, ln_mix_g, ln_mix_b,
              w_ada_ffn, b_ada_ffn, ln_ffn_g, ln_ffn_b, w_in_even, w_out_even,
              cmp_pos_k, cmp_w1_k, cmp_w2_k, cmp_pos_v, cmp_w1_v, cmp_w2_v,
              w_ffn_gate, w_ffn_up, w_ffn_down, w_in_odd, w_out_odd, w_router, b_router,
              w_moe_gate, w_moe_up, w_moe_down):
    N_p, T_p = x_prompt.shape[:2]
    N_s, T_s = x_sample.shape[:2]
    past = page_table.shape[1] * PAGE_SIZE
    pos_p = jnp.arange(T_p)
    pos_s = past + jnp.arange(T_s)
    xp, xs = x_prompt, x_sample
    even_rows_p, even_rows_s, odd_rows_p, odd_rows_s = [], [], [], []
    for layer in range(DEPTH):
        li = layer // 2
        sh_p, sc_p, gt_p = ada(c_prompt, w_ada_mix[layer], b_ada_mix[layer])
        sh_s, sc_s, gt_s = ada(c_sample, w_ada_mix[layer], b_ada_mix[layer])
        hp = xp * (1 + sc_p) + sh_p
        hs = xs * (1 + sc_s) + sh_s
        if layer % 2 == 0:
            cmp_k = (cmp_pos_k[li], cmp_w1_k[li], cmp_w2_k[li])
            cmp_v = (cmp_pos_v[li], cmp_w1_v[li], cmp_w2_v[li])
            (qa, ka, va, qi, ki, wi), (qb, gb, kc, vc, ks, vs, kw, vw) = even_project(hp, pos_p, w_in_even[li])
            oa = dsa_prompt(qa, qi, wi, ka, va, ki, pos_p)
            ob = nsa_prompt(qb, gb, kc, vc, ks, vs, kw, vw, pos_p, cmp_k, cmp_v)
            mp = combine_heads(oa, ob) @ w_out_even[li]
            wlen = min(WINDOW, T_p)
            even_rows_p.append((ka, va, ki, kc, vc, ks, vs, kw[:, -wlen:], vw[:, -wlen:]))
            (qa, ka, va, qi, ki, wi), (qb, gb, kc, vc, ks, vs, kw, vw) = even_project(hs, pos_s, w_in_even[li])
            oa = dsa_sample(qa, qi, wi, ka, va, ki, cache_a_k, cache_a_v, cache_a_kidx, li, page_table, pos_s)
            ob, wk, wv = nsa_sample(qb, gb, kc, vc, ks, vs, kw, vw, pos_s, cache_b_cmp_k, cache_b_cmp_v,
                                    cache_b_slc_k, cache_b_slc_v, state_b_win_k[li], state_b_win_v[li],
                                    li, page_table, cmp_k, cmp_v)
            ms = combine_heads(oa, ob) @ w_out_even[li]
            even_rows_s.append((ka, va, ki, kc, vc, ks, vs, wk, wv))
        else:
            qp, kp, vp = odd_project(hp, w_in_odd[li])
            mp = sb_prompt(qp, kp, vp, pos_p).reshape(N_p, T_p, -1) @ w_out_odd[li]
            odd_rows_p.append((kp, vp))
            qs, kss, vss = odd_project(hs, w_in_odd[li])
            ms = sb_sample(qs, kss, vss, cache_c_k, cache_c_v, li, page_table, pos_s).reshape(N_s, T_s, -1) @ w_out_odd[li]
            odd_rows_s.append((kss, vss))
        xp = layer_norm(ALPHA * xp + gt_p * mp, ln_mix_g[layer], ln_mix_b[layer])
        xs = layer_norm(ALPHA * xs + gt_s * ms, ln_mix_g[layer], ln_mix_b[layer])
        sh_p, sc_p, gt_p = ada(c_prompt, w_ada_ffn[layer], b_ada_ffn[layer])
        sh_s, sc_s, gt_s = ada(c_sample, w_ada_ffn[layer], b_ada_ffn[layer])
        hp = xp * (1 + sc_p) + sh_p
        hs = xs * (1 + sc_s) + sh_s
        if layer % 2 == 0:
            fp = swiglu(hp, w_ffn_gate[li], w_ffn_up[li], w_ffn_down[li])
            fs = swiglu(hs, w_ffn_gate[li], w_ffn_up[li], w_ffn_down[li])
        else:
            fp = moe_ffn(hp, w_router[li], b_router[li], w_moe_gate[li], w_moe_up[li], w_moe_down[li])
            fs = moe_ffn(hs, w_router[li], b_router[li], w_moe_gate[li], w_moe_up[li], w_moe_down[li])
        xp = layer_norm(ALPHA * xp + gt_p * fp, ln_ffn_g[layer], ln_ffn_b[layer])
        xs = layer_norm(ALPHA * xs + gt_s * fs, ln_ffn_g[layer], ln_ffn_b[layer])
    (a_k_p, a_v_p, a_kidx_p, b_cmp_k_p, b_cmp_v_p, b_slc_k_p, b_slc_v_p,
     b_win_k_p, b_win_v_p) = [jnp.stack(t) for t in zip(*even_rows_p)]
    (a_k_s, a_v_s, a_kidx_s, b_cmp_k_s, b_cmp_v_s, b_slc_k_s, b_slc_v_s,
     b_win_k_s, b_win_v_s) = [jnp.stack(t) for t in zip(*even_rows_s)]
    c_k_p, c_v_p = [jnp.stack(t) for t in zip(*odd_rows_p)]
    c_k_s, c_v_s = [jnp.stack(t) for t in zip(*odd_rows_s)]
    return (xp, xs, a_k_p, a_k_s, a_v_p, a_v_s, a_kidx_p, a_kidx_s,
            b_cmp_k_p, b_cmp_k_s, b_cmp_v_p, b_cmp_v_s, b_slc_k_p, b_slc_k_s, b_slc_v_p, b_slc_v_s,
            b_win_k_p, b_win_k_s, b_win_v_p, b_win_v_s, c_k_p, c_k_s, c_v_p, c_v_s)
```

```python
import functools
import math

import numpy as np
import jax
import jax.numpy as jnp
from jax import lax
from jax.experimental import pallas as pl
from jax.experimental.pallas import tpu as pltpu

D_MODEL = 1024
PAGE_SIZE = 128
HEAD_DIM = 64
ROPE_DIM = HEAD_DIM // 4
ROPE_THETA = 500000.0
Q_BLOCK = 128
A_HEADS = 8
A_KV_HEADS = 2
IDX_HEADS = 8
IDX_DIM = 64
IDX_TOPK = 256
B_HEADS = 8
B_KV_HEADS = 2
CMP_LEN = 32
CMP_STRIDE = 16
CMP_HID = 128
SLC_LEN = 64
SLC_TOPN = 16
WINDOW = 512
FORCE_BONUS = 100.0
C_HEADS = 16
D_FF = 2816
N_EXPERTS = 8
TOP_K = 2
D_FF_EXPERT = 3584
DEPTH = 2
ALPHA = (2 * DEPTH) ** 0.25
LN_EPS = 1e-5
NEG = -1e30
EVEN_SPLIT = (A_HEADS * HEAD_DIM, A_KV_HEADS * HEAD_DIM, A_KV_HEADS * HEAD_DIM,
              IDX_HEADS * IDX_DIM, IDX_DIM, IDX_HEADS,
              B_HEADS * HEAD_DIM, 6 * B_KV_HEADS * HEAD_DIM, 3 * B_HEADS)
EVEN_OFFSETS = tuple(int(v) for v in np.cumsum(EVEN_SPLIT)[:-1])

LANES = 128
SUBLANES = 8
VMEM_LIMIT_BYTES = 56 * 1024 * 1024

F32 = jnp.float32
BF16 = jnp.bfloat16
I32 = jnp.int32
INT_MIN = -2 ** 31
HIGHEST = lax.Precision.HIGHEST

C_QA, C_QI, C_QBR, C_KA, C_KS, C_KW, C_KI = 0, 512, 1024, 1536, 1664, 1792, 1920
EV_ROPE_W = 2048
C_QB, C_VA, C_KC, C_VC, C_VS, C_VW, C_WG = 2048, 2560, 2688, 2816, 2944, 3072, 3200
EV_W = 3328
WG_GB_LANE = IDX_HEADS


def _cparams(sem):
    return pltpu.CompilerParams(dimension_semantics=sem, vmem_limit_bytes=VMEM_LIMIT_BYTES)


def _sortable_key(x):
    b = lax.bitcast_convert_type(x + 0.0, I32)
    return jnp.where(b < 0, b ^ jnp.int32(0x7FFFFFFF), b)


_KEY_NEG = int(np.array(NEG, np.float32).view(np.int32)) ^ 0x7FFFFFFF


def _layer_norm_rows(y, g, b):
    mu = jnp.mean(y, axis=-1, keepdims=True)
    d = y - mu
    var = jnp.mean(d * d, axis=-1, keepdims=True)
    return d * lax.rsqrt(var + LN_EPS) * g + b


def _silu(x):
    return x * jax.nn.sigmoid(x)


def _dot_t(a, b):
    return lax.dot_general(a, b, (((1,), (1,)), ((), ())), preferred_element_type=F32)


def _dot(a, b):
    return jnp.dot(a, b, preferred_element_type=F32)


def _topk_bounds(load, n_dyn, n_total, tail_key, k, rows, cw, idx_bits):
    lane = lax.broadcasted_iota(I32, (rows, cw), 1)
    n_tail = ((n_total - n_dyn) * cw).astype(F32) if not isinstance(n_dyn, int) else float((n_total - n_dyn) * cw)
    dyn_w = n_dyn * cw
    kf = float(k)

    def count(pred):
        def body(c, tot):
            keys = load(c)
            return tot + jnp.sum(pred(keys, lane + c * cw).astype(F32), axis=1, keepdims=True)
        return lax.fori_loop(0, n_dyn, body, jnp.zeros((rows, 1), F32))

    def bit_body(it, u):
        bit = lax.shift_left(jnp.int32(1), jnp.int32(31) - it)
        cu = u | bit
        cand = cu ^ jnp.int32(INT_MIN)
        cnt = count(lambda keys, idx: keys >= cand) + n_tail * (jnp.int32(tail_key) >= cand).astype(F32)
        return jnp.where(cnt >= kf, cu, u)

    u = lax.fori_loop(0, 32, bit_body, jnp.zeros((rows, 1), I32))
    thr = u ^ jnp.int32(INT_MIN)
    tail_gt = (jnp.int32(tail_key) > thr).astype(F32)
    tail_eq = (jnp.int32(tail_key) == thr).astype(F32)
    c_gt = count(lambda keys, idx: keys > thr) + n_tail * tail_gt
    need = kf - c_gt

    def j_body(it, j0):
        bit = lax.shift_left(jnp.int32(1), jnp.int32(idx_bits - 1) - it)
        cj = j0 | bit
        f = count(lambda keys, idx: (keys == thr) & (idx < cj))
        f = f + tail_eq * jnp.clip((cj - dyn_w).astype(F32), 0.0, n_tail)
        return jnp.where(f < need, cj, j0)

    j0 = lax.fori_loop(0, idx_bits, j_body, jnp.zeros((rows, 1), I32))
    return thr, j0


def _ada_kernel(c_ref, w_ref, b_ref, o_ref):
    s = _silu(c_ref[...])
    o_ref[0] = jnp.dot(s, w_ref[0], precision=HIGHEST, preferred_element_type=F32) + b_ref[0]


def ada_all(c_all, w, b):
    mp, d = c_all.shape
    nl, _, n3 = w.shape
    tn = 512
    return pl.pallas_call(
        _ada_kernel,
        grid=(nl, n3 // tn),
        in_specs=[pl.BlockSpec((mp, d), lambda l, j: (0, 0)),
                  pl.BlockSpec((1, d, tn), lambda l, j: (l, 0, j)),
                  pl.BlockSpec((1, 1, tn), lambda l, j: (l, 0, j))],
        out_specs=pl.BlockSpec((1, mp, tn), lambda l, j: (l, 0, j)),
        out_shape=jax.ShapeDtypeStruct((nl, mp, n3), F32),
        compiler_params=_cparams(("arbitrary", "arbitrary")),
        name="ada",
    )(c_all, w, b.reshape(nl, 1, n3))


def _proj_kernel(x_ref, sc_ref, sh_ref, w_ref, cos_ref, sn_ref, sp_ref, o_ref, *, n_rope_tiles, tn):
    j = pl.program_id(1)
    h = (x_ref[...] * (1.0 + sc_ref[0]) + sh_ref[0]).astype(BF16)
    acc = _dot(h, w_ref[...])

    @pl.when(j < n_rope_tiles)
    def _():
        c, sn, sp = cos_ref[...], sn_ref[...], sp_ref[...]
        for k in range(tn // LANES):
            seg = acc[:, k * LANES:(k + 1) * LANES]
            o_ref[:, k * LANES:(k + 1) * LANES] = (
                seg * c + pltpu.roll(seg, LANES - ROPE_DIM // 2, 1) * sn + pltpu.roll(seg, ROPE_DIM // 2, 1) * sp)

    @pl.when(j >= n_rope_tiles)
    def _():
        o_ref[...] = acc


def mod_project(x, sc, sh, w, rope_tabs, rope_width, seq, tm, tn):
    m, d = x.shape
    n = w.shape[1]
    cos_t, sn_t, sp_t = rope_tabs
    tiles_per_seq = seq // tm
    mod_spec = pl.BlockSpec((1, 1, d), lambda i, j: (i // tiles_per_seq, 0, 0))
    tab_spec = pl.BlockSpec((tm, LANES), lambda i, j: (i % tiles_per_seq, 0))
    return pl.pallas_call(
        functools.partial(_proj_kernel, n_rope_tiles=rope_width // tn, tn=tn),
        grid=(m // tm, n // tn),
        in_specs=[pl.BlockSpec((tm, d), lambda i, j: (i, 0)), mod_spec, mod_spec,
                  pl.BlockSpec((d, tn), lambda i, j: (0, j)), tab_spec, tab_spec, tab_spec],
        out_specs=pl.BlockSpec((tm, tn), lambda i, j: (i, j)),
        out_shape=jax.ShapeDtypeStruct((m, n), F32),
        compiler_params=_cparams(("parallel", "arbitrary")),
        name="mod_project",
    )(x, sc, sh, w, cos_t, sn_t, sp_t)


def _outproj_ln_kernel(a1_ref, a2_ref, w_ref, x_ref, gt_ref, g_ref, b_ref, o_ref):
    a = jnp.concatenate([a1_ref[...], a2_ref[...]], axis=1).astype(BF16)
    y = ALPHA * x_ref[...] + gt_ref[0] * _dot(a, w_ref[...])
    o_ref[...] = _layer_norm_rows(y, g_ref[...], b_ref[...])


def outproj_ln(a1, a1_blk, a2, a2_blk, w, x, gt, ln_g, ln_b, seq, tm):
    m, d = x.shape
    half = d // 2
    tiles_per_seq = seq // tm
    return pl.pallas_call(
        _outproj_ln_kernel,
        grid=(m // tm,),
        in_specs=[pl.BlockSpec((tm, half), lambda i: (i, a1_blk)),
                  pl.BlockSpec((tm, half), lambda i: (i, a2_blk)),
                  pl.BlockSpec((d, d), lambda i: (0, 0)),
                  pl.BlockSpec((tm, d), lambda i: (i, 0)),
                  pl.BlockSpec((1, 1, d), lambda i: (i // tiles_per_seq, 0, 0)),
                  pl.BlockSpec((1, d), lambda i: (0, 0)),
                  pl.BlockSpec((1, d), lambda i: (0, 0))],
        out_specs=pl.BlockSpec((tm, d), lambda i: (i, 0)),
        out_shape=jax.ShapeDtypeStruct((m, d), F32),
        compiler_params=_cparams(("parallel",)),
        name="outproj_ln",
    )(a1, a2, w, x, gt, ln_g.reshape(1, d), ln_b.reshape(1, d))


def _ffn_ln_kernel(x_ref, sc_ref, sh_ref, gt_ref, wg_ref, wu_ref, wd_ref, g_ref, b_ref, o_ref, h_scr, acc_scr):
    j = pl.program_id(1)

    @pl.when(j == 0)
    def _():
        h_scr[...] = (x_ref[...] * (1.0 + sc_ref[0]) + sh_ref[0]).astype(BF16)
        acc_scr[...] = jnp.zeros_like(acc_scr)

    h = h_scr[...]
    a = (_silu(_dot(h, wg_ref[...])) * _dot(h, wu_ref[...])).astype(BF16)
    acc_scr[...] += _dot(a, wd_ref[...])

    @pl.when(j == pl.num_programs(1) - 1)
    def _():
        y = ALPHA * x_ref[...] + gt_ref[0] * acc_scr[...]
        o_ref[...] = _layer_norm_rows(y, g_ref[...], b_ref[...])


def ffn_ln(x, sc, sh, gt, wg, wu, wd, ln_g, ln_b, seq, tm, tf):
    m, d = x.shape
    f = wg.shape[1]
    tiles_per_seq = seq // tm
    mod_spec = pl.BlockSpec((1, 1, d), lambda i, j: (i // tiles_per_seq, 0, 0))
    vec_spec = pl.BlockSpec((1, d), lambda i, j: (0, 0))
    return pl.pallas_call(
        _ffn_ln_kernel,
        grid=(m // tm, f // tf),
        in_specs=[pl.BlockSpec((tm, d), lambda i, j: (i, 0)), mod_spec, mod_spec, mod_spec,
                  pl.BlockSpec((d, tf), lambda i, j: (0, j)),
                  pl.BlockSpec((d, tf), lambda i, j: (0, j)),
                  pl.BlockSpec((tf, d), lambda i, j: (j, 0)), vec_spec, vec_spec],
        out_specs=pl.BlockSpec((tm, d), lambda i, j: (i, 0)),
        out_shape=jax.ShapeDtypeStruct((m, d), F32),
        scratch_shapes=[pltpu.VMEM((tm, d), BF16), pltpu.VMEM((tm, d), F32)],
        compiler_params=_cparams(("parallel", "arbitrary")),
        name="ffn_ln",
    )(x, sc, sh, gt, wg, wu, wd, ln_g.reshape(1, d), ln_b.reshape(1, d))


def _head_tile(x, h, dst):
    t = x[:, (h // 2) * LANES:(h // 2 + 1) * LANES]
    if h % 2 != dst:
        t = pltpu.roll(t, HEAD_DIM, 1)
    half = lax.broadcasted_iota(I32, t.shape, 1) // HEAD_DIM
    return jnp.where(half == dst, t, 0.0)


def _stack_group(x, g, heads_per_group, scale):
    tiles = [_head_tile(x, g * heads_per_group + r, g) * scale for r in range(heads_per_group)]
    return jnp.concatenate(tiles, axis=0).astype(BF16)


def _unstack_groups(o_groups, heads_per_group, q):
    n_heads = len(o_groups) * heads_per_group
    lane_half = lax.broadcasted_iota(I32, (q, LANES), 1) // HEAD_DIM
    pairs = []
    for hp in range(n_heads // 2):
        halves = []
        for pos in range(2):
            h = 2 * hp + pos
            g, r = divmod(h, heads_per_group)
            t = o_groups[g][r * q:(r + 1) * q]
            if g != pos:
                t = pltpu.roll(t, HEAD_DIM, 1)
            halves.append(t)
        pairs.append(jnp.where(lane_half == 0, halves[0], halves[1]))
    return jnp.concatenate(pairs, axis=1)


def _flash_step(qg, kc, vc, mk, carry):
    m, l, acc = carry
    s = _dot_t(qg, kc)
    s = jnp.where(mk > 0.5, s, NEG)
    m_new = jnp.maximum(m, jnp.max(s, axis=1, keepdims=True))
    alpha = jnp.exp(m - m_new)
    p = jnp.exp(s - m_new) * mk
    l = alpha * l + jnp.sum(p, axis=1, keepdims=True)
    acc = alpha * acc + _dot(p.astype(BF16), vc)
    return m_new, l, acc


def _flash_init(rows):
    return (jnp.full((rows, 1), NEG, F32), jnp.zeros((rows, 1), F32), jnp.zeros((rows, LANES), F32))


def _flash_out(carry):
    _, l, acc = carry
    return acc / jnp.maximum(l, 1e-30)


DSA_CW = 512


def _dsa_kernel(qa_ref, qi_ref, wg_ref, ki_ref, ka_ref, va_ref, o_ref, key_scr, msk_scr, *, seq, topk):
    q = Q_BLOCK
    i = pl.program_id(1)
    q0 = i * q
    nch = seq // DSA_CW
    n_dyn = (q0 + q + DSA_CW - 1) // DSA_CW
    row = q0 + lax.broadcasted_iota(I32, (q, 1), 0)
    lane = lax.broadcasted_iota(I32, (q, DSA_CW), 1)

    qi = qi_ref[...]
    wi = wg_ref[...]
    qih = []
    for h in range(IDX_HEADS):
        t = qi[:, (h // 2) * LANES:(h // 2 + 1) * LANES]
        if h % 2:
            t = pltpu.roll(t, HEAD_DIM, 1)
        qih.append(t.astype(BF16))

    def score_body(c, _):
        off = pl.multiple_of(c * DSA_CW, DSA_CW)
        kc = ki_ref[pl.ds(off, DSA_CW), :].astype(BF16)
        acc = jnp.zeros((q, DSA_CW), F32)
        for h in range(IDX_HEADS):
            acc = acc + wi[:, h:h + 1] * jnp.maximum(_dot_t(qih[h], kc), 0.0)
        acc = jnp.where(lane + off <= row, acc, NEG)
        key_scr[c] = _sortable_key(acc)
        return 0

    lax.fori_loop(0, n_dyn, score_body, 0)

    thr, j0 = _topk_bounds(lambda c: key_scr[c], n_dyn, nch, _KEY_NEG, topk, q, DSA_CW,
                           int(math.log2(seq)))

    def mask_body(c, _):
        keys = key_scr[c]
        idx = lane + c * DSA_CW
        sel = (keys > thr) | ((keys == thr) & (idx <= j0))
        msk_scr[c] = (sel & (idx <= row)).astype(F32)
        return 0

    lax.fori_loop(0, n_dyn, mask_body, 0)

    qa = qa_ref[...]
    hpg = A_HEADS // A_KV_HEADS
    outs = []
    for g in range(A_KV_HEADS):
        qg = _stack_group(qa, g, hpg, 1.0 / math.sqrt(HEAD_DIM))

        def attn_body(c, carry, qg=qg):
            off = pl.multiple_of(c * DSA_CW, DSA_CW)
            kc = ka_ref[pl.ds(off, DSA_CW), :].astype(BF16)
            vc = va_ref[pl.ds(off, DSA_CW), :].astype(BF16)
            mk = msk_scr[c]
            mk = jnp.concatenate([mk] * hpg, axis=0)
            return _flash_step(qg, kc, vc, mk, carry)

        outs.append(_flash_out(lax.fori_loop(0, n_dyn, attn_body, _flash_init(hpg * q))))
    o_ref[...] = _unstack_groups(outs, hpg, q)


def dsa_prompt(p, n_seq, seq):
    nq = seq // Q_BLOCK
    topk = min(IDX_TOPK, seq // 4)
    qspec = lambda cb: pl.BlockSpec((Q_BLOCK, 512), lambda n, i: (n * nq + i, cb))
    kvspec = lambda cb: pl.BlockSpec((seq, LANES), lambda n, i: (n, cb))
    return pl.pallas_call(
        functools.partial(_dsa_kernel, seq=seq, topk=topk),
        grid=(n_seq, nq),
        in_specs=[qspec(C_QA // 512), qspec(C_QI // 512),
                  pl.BlockSpec((Q_BLOCK, LANES), lambda n, i: (n * nq + i, C_WG // LANES)),
                  kvspec(C_KI // LANES), kvspec(C_KA // LANES), kvspec(C_VA // LANES)],
        out_specs=pl.BlockSpec((Q_BLOCK, 512), lambda n, i: (n * nq + i, 0)),
        out_shape=jax.ShapeDtypeStruct((n_seq * seq, 512), F32),
        scratch_shapes=[pltpu.VMEM((seq // DSA_CW, Q_BLOCK, DSA_CW), I32),
                        pltpu.VMEM((seq // DSA_CW, Q_BLOCK, DSA_CW), F32)],
        compiler_params=_cparams(("parallel", "arbitrary")),
        name="dsa_prompt",
    )(p, p, p, p, p, p)


def _compress_kernel(x_ref, pa_ref, pb_ref, w1a_ref, w1b_ref, w2_ref, o_ref):
    x = x_ref[0]
    nchunk = x.shape[0]
    a = _dot((x + pa_ref[...]).astype(BF16), w1a_ref[...])
    b = _dot((x + pb_ref[...]).astype(BF16), w1b_ref[...])
    h = a + pltpu.roll(b, nchunk - 1, 0)
    o_ref[0] = _dot(_silu(h).astype(BF16), w2_ref[...])


def compress_rows(rows, pos_emb, w1, w2):
    n, l, gd = rows.shape
    g = gd // HEAD_DIM
    nchunk = l // CMP_STRIDE
    x = rows.reshape(n, nchunk, CMP_STRIDE * gd)
    r = CMP_LEN // CMP_STRIDE
    w1c = w1.reshape(r, CMP_STRIDE, HEAD_DIM, CMP_HID)
    eye = jnp.eye(g, dtype=F32)
    big = [jnp.einsum('ldh,gk->lgdkh', w1c[j], eye).reshape(CMP_STRIDE * gd, g * CMP_HID).astype(BF16) for j in range(r)]
    w2big = jnp.einsum('hd,gk->ghkd', w2, eye).reshape(g * CMP_HID, gd).astype(BF16)
    pe = pos_emb.reshape(r, CMP_STRIDE, 1, HEAD_DIM)
    pab = [jnp.broadcast_to(pe[j], (CMP_STRIDE, g, HEAD_DIM)).reshape(1, CMP_STRIDE * gd) for j in range(r)]
    k2 = CMP_STRIDE * gd
    return pl.pallas_call(
        _compress_kernel,
        grid=(n,),
        in_specs=[pl.BlockSpec((1, nchunk, k2), lambda i: (i, 0, 0)),
                  pl.BlockSpec((1, k2), lambda i: (0, 0)), pl.BlockSpec((1, k2), lambda i: (0, 0)),
                  pl.BlockSpec((k2, g * CMP_HID), lambda i: (0, 0)),
                  pl.BlockSpec((k2, g * CMP_HID), lambda i: (0, 0)),
                  pl.BlockSpec((g * CMP_HID, gd), lambda i: (0, 0))],
        out_specs=pl.BlockSpec((1, nchunk, gd), lambda i: (i, 0, 0)),
        out_shape=jax.ShapeDtypeStruct((n, nchunk, gd), F32),
        compiler_params=_cparams(("parallel",)),
        name="compress",
    )(x, pab[0], pab[1], big[0], big[1], w2big)


NSA_CW = 512


def _nsa_kernel(qb_ref, qr_ref, wg_ref, kcc_ref, vcc_ref, ks_ref, vs_ref, kw_ref, vw_ref, o_ref, *, seq):
    q = Q_BLOCK
    i = pl.program_id(1)
    q0 = i * q
    hpg = B_HEADS // B_KV_HEADS
    ncmp = seq // CMP_STRIDE
    nslc = seq // SLC_LEN
    n_dyn = (q0 + q + NSA_CW - 1) // NSA_CW
    scale = 1.0 / math.sqrt(HEAD_DIM)
    row = q0 + lax.broadcasted_iota(I32, (q, 1), 0)

    qb = qb_ref[...]
    qr = qr_ref[...]
    kcc = kcc_ref[0].astype(BF16)
    vcc = vcc_ref[0].astype(BF16)

    cidx = lax.broadcasted_iota(I32, (q, ncmp), 1)
    cmask = (cidx * CMP_STRIDE + (CMP_LEN - 1) <= row).astype(F32)
    cmask_g = jnp.concatenate([cmask] * hpg, axis=0)
    assert nslc <= LANES
    ci = lax.broadcasted_iota(I32, (ncmp, LANES), 0) * CMP_STRIDE
    sj = lax.broadcasted_iota(I32, (ncmp, LANES), 1) * SLC_LEN
    cmp2slc = ((ci < sj + SLC_LEN) & (ci + CMP_LEN > sj) & (sj < seq)).astype(BF16)
    blk = lax.broadcasted_iota(I32, (q, LANES), 1)
    cur = row // SLC_LEN
    forced = ((blk == 0) | (blk == cur) | (blk == cur - 1)).astype(F32)
    avail = (blk * SLC_LEN <= row) & (blk < nslc)

    o_cmp, sel_keys = [], []
    for g in range(B_KV_HEADS):
        qg = _stack_group(qb, g, hpg, scale)
        s = jnp.where(cmask_g > 0.5, _dot_t(qg, kcc), NEG)
        e = jnp.exp(s - jnp.max(s, axis=1, keepdims=True)) * cmask_g
        p = e / jnp.maximum(jnp.sum(e, axis=1, keepdims=True), 1e-30)
        o_cmp.append(_dot(p.astype(BF16), vcc))
        psum = p[0:q]
        for r in range(1, hpg):
            psum = psum + p[r * q:(r + 1) * q]
        p_hi = psum.astype(BF16)
        p_lo = (psum - p_hi.astype(F32)).astype(BF16)
        p_slc = _dot(p_hi, cmp2slc) + _dot(p_lo, cmp2slc)
        score = jnp.where(avail, p_slc + FORCE_BONUS * forced, NEG)
        sel_keys.append(jnp.where(blk < nslc, _sortable_key(score), jnp.int32(INT_MIN)))

    keys = jnp.concatenate(sel_keys, axis=0)
    ntop = min(SLC_TOPN, nslc)
    thr, j0 = _topk_bounds(lambda c: keys, 1, 1, INT_MIN, ntop, B_KV_HEADS * q, LANES, int(math.log2(LANES)))
    lane_b = lax.broadcasted_iota(I32, keys.shape, 1)
    sel = ((keys > thr) | ((keys == thr) & (lane_b <= j0))).astype(BF16)

    lane = lax.broadcasted_iota(I32, (q, NSA_CW), 1)
    eb = lax.broadcasted_iota(I32, (LANES, NSA_CW), 0)
    es = lax.broadcasted_iota(I32, (LANES, NSA_CW), 1)

    o_slc, o_win = [], []
    for g in range(B_KV_HEADS):
        qg = _stack_group(qr, g, hpg, scale)
        bm = sel[g * q:(g + 1) * q]

        def slc_body(c, carry, qg=qg, bm=bm):
            off = pl.multiple_of(c * NSA_CW, NSA_CW)
            kc = ks_ref[pl.ds(off, NSA_CW), :].astype(BF16)
            vc = vs_ref[pl.ds(off, NSA_CW), :].astype(BF16)
            expand = ((es + off) // SLC_LEN == eb).astype(BF16)
            mk = _dot(bm, expand) * (lane + off <= row).astype(F32)
            mk = jnp.concatenate([mk] * hpg, axis=0)
            return _flash_step(qg, kc, vc, mk, carry)

        o_slc.append(_flash_out(lax.fori_loop(0, n_dyn, slc_body, _flash_init(hpg * q))))

        lane_w = lax.broadcasted_iota(I32, (q, q), 1)

        def win_body(kb, carry, qg=qg):
            off = pl.multiple_of(kb * q, q)
            kc = kw_ref[pl.ds(off, q), :].astype(BF16)
            vc = vw_ref[pl.ds(off, q), :].astype(BF16)
            col = lane_w + off
            mk = ((col <= row) & (col > row - WINDOW)).astype(F32)
            mk = jnp.concatenate([mk] * hpg, axis=0)
            return _flash_step(qg, kc, vc, mk, carry)

        kb_lo = jnp.maximum(i - WINDOW // q, 0)
        o_win.append(_flash_out(lax.fori_loop(kb_lo, i + 1, win_body, _flash_init(hpg * q))))

    gate = jax.nn.sigmoid(wg_ref[...])
    outs = []
    for g in range(B_KV_HEADS):
        cols = [[gate[:, WG_GB_LANE + (g * hpg + r) * 3 + b: WG_GB_LANE + (g * hpg + r) * 3 + b + 1]
                 for r in range(hpg)] for b in range(3)]
        gc = [jnp.concatenate(cols[b], axis=0) for b in range(3)]
        outs.append(gc[0] * o_cmp[g] + gc[1] * o_slc[g] + gc[2] * o_win[g])
    o_ref[...] = _unstack_groups(outs, hpg, q)


def nsa_prompt(p, kcc, vcc, n_seq, seq):
    nq = seq // Q_BLOCK
    ncmp = seq // CMP_STRIDE
    qspec = lambda cb: pl.BlockSpec((Q_BLOCK, 512), lambda n, i: (n * nq + i, cb))
    kvspec = lambda cb: pl.BlockSpec((seq, LANES), lambda n, i: (n, cb))
    cspec = pl.BlockSpec((1, ncmp, LANES), lambda n, i: (n, 0, 0))
    return pl.pallas_call(
        functools.partial(_nsa_kernel, seq=seq),
        grid=(n_seq, nq),
        in_specs=[qspec(C_QB // 512), qspec(C_QBR // 512),
                  pl.BlockSpec((Q_BLOCK, LANES), lambda n, i: (n * nq + i, C_WG // LANES)),
                  cspec, cspec,
                  kvspec(C_KS // LANES), kvspec(C_VS // LANES), kvspec(C_KW // LANES), kvspec(C_VW // LANES)],
        out_specs=pl.BlockSpec((Q_BLOCK, 512), lambda n, i: (n * nq + i, 0)),
        out_shape=jax.ShapeDtypeStruct((n_seq * seq, 512), F32),
        compiler_params=_cparams(("parallel", "arbitrary")),
        name="nsa_prompt",
    )(p, p, p, kcc, vcc, p, p, p, p)


SB_DEAD = -110.0


def _log_sigmoid_pair(z):
    ls = jnp.minimum(z, 0.0) - jnp.log(1.0 + jnp.exp(-jnp.abs(z)))
    return ls, ls - z


def _split_bf16(x):
    hi = x.astype(BF16)
    return hi, (x - hi.astype(F32)).astype(BF16)


def _sb_kernel(q_ref, k_ref, v_ref, o_ref):
    q = Q_BLOCK
    i = pl.program_id(2)
    q0 = i * q
    row = q0 + lax.broadcasted_iota(I32, (q, 1), 0)
    lane = lax.broadcasted_iota(I32, (q, q), 1)
    tri = (lax.broadcasted_iota(I32, (q, q), 0) > lane).astype(BF16)
    half = lax.broadcasted_iota(I32, (q, LANES), 1) // HEAD_DIM
    qq = q_ref[...] * (1.0 / math.sqrt(HEAD_DIM))
    qsplit = [_split_bf16(jnp.where(half == h, qq, 0.0)) for h in range(2)]

    def cond(st):
        j, c0, c1, _, _ = st
        return (j >= 0) & (jnp.max(jnp.maximum(c0, c1)) > SB_DEAD)

    def body(st):
        j, c0, c1, a0, a1 = st
        off = pl.multiple_of(j * q, q)
        k_hi, k_lo = _split_bf16(k_ref[pl.ds(off, q), :])
        vb = v_ref[pl.ds(off, q), :].astype(BF16)
        mask = lane + off < row
        carries, accs = [c0, c1], [a0, a1]
        for h in range(2):
            q_hi, q_lo = qsplit[h]
            z = _dot_t(q_hi, k_hi) + _dot_t(q_hi, k_lo) + _dot_t(q_lo, k_hi)
            ls, lsn = _log_sigmoid_pair(z)
            u = jnp.where(mask, lsn, 0.0)
            u_hi, u_lo = _split_bf16(u)
            later = _dot(u_hi, tri) + _dot(u_lo, tri) + carries[h]
            a = jnp.where(mask, jnp.exp(ls + later), 0.0)
            accs[h] = accs[h] + _dot(a.astype(BF16), vb)
            carries[h] = carries[h] + jnp.sum(u, axis=1, keepdims=True)
        return j - 1, carries[0], carries[1], accs[0], accs[1]

    z1 = jnp.zeros((q, 1), F32)
    zl = jnp.zeros((q, LANES), F32)
    _, _, _, a0, a1 = lax.while_loop(cond, body, (i, z1, z1, zl, zl))
    o_ref[...] = jnp.where(half == 0, a0, a1)


def sb_prompt(p1, n_seq, seq):
    nq = seq // Q_BLOCK
    npair = C_HEADS // 2
    return pl.pallas_call(
        _sb_kernel,
        grid=(n_seq, npair, nq),
        in_specs=[pl.BlockSpec((Q_BLOCK, LANES), lambda n, hp, i: (n * nq + i, hp)),
                  pl.BlockSpec((seq, LANES), lambda n, hp, i: (n, npair + hp)),
                  pl.BlockSpec((seq, LANES), lambda n, hp, i: (n, 2 * npair + hp))],
        out_specs=pl.BlockSpec((Q_BLOCK, LANES), lambda n, hp, i: (n * nq + i, hp)),
        out_shape=jax.ShapeDtypeStruct((n_seq * seq, C_HEADS * HEAD_DIM), F32),
        compiler_params=_cparams(("parallel", "parallel", "arbitrary")),
        name="sb_prompt",
    )(p1, p1, p1)


MOE_TM = 512
MOE_TF = 1792


def _router_kernel(x_ref, sc_ref, sh_ref, wr_ref, br_ref, h_ref, r_ref):
    h = x_ref[...] * (1.0 + sc_ref[0]) + sh_ref[0]
    h_ref[...] = h
    logits = jnp.dot(h, wr_ref[...], precision=HIGHEST, preferred_element_type=F32) + br_ref[...]
    lane = lax.broadcasted_iota(I32, logits.shape, 1)
    logits = jnp.where(lane < N_EXPERTS, logits, NEG)
    m1 = jnp.max(logits, axis=1, keepdims=True)
    lane_f = lane.astype(F32)
    i1 = jnp.min(jnp.where(logits == m1, lane_f, float(LANES)), axis=1, keepdims=True)
    rest = jnp.where(lane_f == i1, NEG, logits)
    m2 = jnp.max(rest, axis=1, keepdims=True)
    i2 = jnp.min(jnp.where(rest == m2, lane_f, float(LANES)), axis=1, keepdims=True)
    e2 = jnp.exp(m2 - m1)
    g1 = 1.0 / (1.0 + e2)
    g2 = e2 / (1.0 + e2)
    r_ref[...] = (jnp.where(lane == 0, i1, 0.0) + jnp.where(lane == 1, i2, 0.0)
                  + jnp.where(lane == 2, g1, 0.0) + jnp.where(lane == 3, g2, 0.0))


def router(x, sc, sh, w_router, b_router, seq, tm):
    m, d = x.shape
    tiles_per_seq = seq // tm
    wr = jnp.zeros((d, LANES), F32).at[:, :N_EXPERTS].set(w_router)
    br = jnp.zeros((1, LANES), F32).at[0, :N_EXPERTS].set(b_router)
    mod_spec = pl.BlockSpec((1, 1, d), lambda i: (i // tiles_per_seq, 0, 0))
    return pl.pallas_call(
        _router_kernel,
        grid=(m // tm,),
        in_specs=[pl.BlockSpec((tm, d), lambda i: (i, 0)), mod_spec, mod_spec,
                  pl.BlockSpec((d, LANES), lambda i: (0, 0)), pl.BlockSpec((1, LANES), lambda i: (0, 0))],
        out_specs=[pl.BlockSpec((tm, d), lambda i: (i, 0)), pl.BlockSpec((tm, LANES), lambda i: (i, 0))],
        out_shape=[jax.ShapeDtypeStruct((m, d), F32), jax.ShapeDtypeStruct((m, LANES), F32)],
        compiler_params=_cparams(("parallel",)),
        name="router",
    )(x, sc, sh, wr, br)


def _gather_rows(idx_ref, src_hbm, dst, sem, n):
    def issue(r, c):
        pltpu.make_async_copy(src_hbm.at[pl.ds(idx_ref[r], 1)], dst.at[pl.ds(r, 1)], sem).start()
        return c
    lax.fori_loop(0, n, issue, 0)

    def drain(r, c):
        pltpu.make_async_copy(src_hbm.at[pl.ds(0, 1)], dst.at[pl.ds(0, 1)], sem).wait()
        return c
    lax.fori_loop(0, n, drain, 0)


def _moe_ffn_kernel(te_ref, nu_ref, tok_ref, h_hbm, wg_ref, wu_ref, wd_ref, o_ref, xbuf, xb16, acc, sem):
    i = pl.program_id(0)
    j = pl.program_id(1)
    last = pl.num_programs(1) - 1
    live = i < nu_ref[0]

    @pl.when(live)
    def _():
        @pl.when(j == 0)
        def _():
            _gather_rows(tok_ref, h_hbm, xbuf, sem, MOE_TM)
            xb16[...] = xbuf[...].astype(BF16)
            acc[...] = jnp.zeros_like(acc)

        x = xb16[...]
        a = (_silu(_dot(x, wg_ref[0])) * _dot(x, wu_ref[0])).astype(BF16)
        acc[...] += _dot(a, wd_ref[0])

        @pl.when(j == last)
        def _():
            o_ref[...] = acc[...]

    @pl.when(jnp.logical_not(live) & (j == last))
    def _():
        o_ref[...] = jnp.zeros_like(o_ref)


def moe_expert_rows(h, tok_of_slot, tile_expert, n_used, wg, wu, wd):
    d = h.shape[1]
    s_pad = tok_of_slot.shape[0]
    n_tiles = s_pad // MOE_TM
    n_ff = D_FF_EXPERT // MOE_TF
    grid_spec = pltpu.PrefetchScalarGridSpec(
        num_scalar_prefetch=2,
        grid=(n_tiles, n_ff),
        in_specs=[pl.BlockSpec((MOE_TM,), lambda i, j, te, nu: (i,), memory_space=pltpu.SMEM),
                  pl.BlockSpec(memory_space=pl.ANY),
                  pl.BlockSpec((1, d, MOE_TF), lambda i, j, te, nu: (te[i], 0, j)),
                  pl.BlockSpec((1, d, MOE_TF), lambda i, j, te, nu: (te[i], 0, j)),
                  pl.BlockSpec((1, MOE_TF, d), lambda i, j, te, nu: (te[i], j, 0))],
        out_specs=pl.BlockSpec((MOE_TM, d), lambda i, j, te, nu: (i, 0)),
        scratch_shapes=[pltpu.VMEM((MOE_TM, d), F32), pltpu.VMEM((MOE_TM, d), BF16),
                        pltpu.VMEM((MOE_TM, d), F32), pltpu.SemaphoreType.DMA(())],
    )
    return pl.pallas_call(
        _moe_ffn_kernel,
        grid_spec=grid_spec,
        out_shape=jax.ShapeDtypeStruct((s_pad, d), F32),
        compiler_params=_cparams(("arbitrary", "arbitrary")),
        name="moe_ffn",
    )(tile_expert, n_used, tok_of_slot, h, wg, wu, wd)


MOE_TC = 256


def _moe_combine_kernel(s0_ref, s1_ref, ys_hbm, r_ref, x_ref, gt_ref, g_ref, b_ref, o_ref, y0, y1, sem):
    _gather_rows(s0_ref, ys_hbm, y0, sem, MOE_TC)
    _gather_rows(s1_ref, ys_hbm, y1, sem, MOE_TC)
    r = r_ref[...]
    f = r[:, 2:3] * y0[...] + r[:, 3:4] * y1[...]
    y = ALPHA * x_ref[...] + gt_ref[0] * f
    o_ref[...] = _layer_norm_rows(y, g_ref[...], b_ref[...])


def moe_combine_ln(ys, slot0, slot1, r, x, gt, ln_g, ln_b, seq):
    m, d = x.shape
    tiles_per_seq = seq // MOE_TC
    return pl.pallas_call(
        _moe_combine_kernel,
        grid=(m // MOE_TC,),
        in_specs=[pl.BlockSpec((MOE_TC,), lambda i: (i,), memory_space=pltpu.SMEM),
                  pl.BlockSpec((MOE_TC,), lambda i: (i,), memory_space=pltpu.SMEM),
                  pl.BlockSpec(memory_space=pl.ANY),
                  pl.BlockSpec((MOE_TC, LANES), lambda i: (i, 0)),
                  pl.BlockSpec((MOE_TC, d), lambda i: (i, 0)),
                  pl.BlockSpec((1, 1, d), lambda i: (i // tiles_per_seq, 0, 0)),
                  pl.BlockSpec((1, d), lambda i: (0, 0)), pl.BlockSpec((1, d), lambda i: (0, 0))],
        out_specs=pl.BlockSpec((MOE_TC, d), lambda i: (i, 0)),
        out_shape=jax.ShapeDtypeStruct((m, d), F32),
        scratch_shapes=[pltpu.VMEM((MOE_TC, d), F32), pltpu.VMEM((MOE_TC, d), F32), pltpu.SemaphoreType.DMA(())],
        compiler_params=_cparams(("arbitrary",)),
        name="moe_combine_ln",
    )(slot0, slot1, ys, r, x, gt, ln_g.reshape(1, d), ln_b.reshape(1, d))


def moe_routing_tables(r):
    t = r.shape[0]
    experts = r[:, :TOP_K].astype(I32).reshape(-1)
    onehot = (experts[:, None] == jnp.arange(N_EXPERTS, dtype=I32)[None, :]).astype(I32)
    rank = jnp.sum((jnp.cumsum(onehot, axis=0) - onehot) * onehot, axis=1)
    counts = jnp.sum(onehot, axis=0)
    padded = ((counts + MOE_TM - 1) // MOE_TM) * MOE_TM
    ends = jnp.cumsum(padded)
    starts = ends - padded
    slot = starts[experts] + rank
    n_tiles = -(-(TOP_K * t) // MOE_TM) + N_EXPERTS
    s_pad = n_tiles * MOE_TM
    tok_of_slot = jnp.zeros((s_pad,), I32).at[slot].set(jnp.arange(TOP_K * t, dtype=I32) // TOP_K)
    tile_start = jnp.arange(n_tiles, dtype=I32) * MOE_TM
    tile_expert = jnp.minimum(jnp.sum((tile_start[:, None] >= ends[None, :]).astype(I32), axis=1), N_EXPERTS - 1)
    n_used = (ends[-1] // MOE_TM).astype(I32).reshape(1)
    slot2 = slot.reshape(t, TOP_K)
    return tok_of_slot, tile_expert.astype(I32), n_used, slot2[:, 0], slot2[:, 1]


def rope_tables(pos):
    half = ROPE_DIM // 2
    inv = ROPE_THETA ** (-jnp.arange(half, dtype=F32) / half)
    ang = pos.astype(F32)[:, None] * inv[None, :]
    cos, sin = jnp.cos(ang), jnp.sin(ang)
    ones = jnp.ones((pos.shape[0], HEAD_DIM - ROPE_DIM), F32)
    zeros = jnp.zeros_like(ones)
    zh = jnp.zeros_like(sin)
    c64 = jnp.concatenate([cos, cos, ones], axis=1)
    sn64 = jnp.concatenate([-sin, zh, zeros], axis=1)
    sp64 = jnp.concatenate([zh, sin, zeros], axis=1)
    return tuple(jnp.concatenate([t, t], axis=1) for t in (c64, sn64, sp64))


def _even_weight_layout(w_in):
    o = (0,) + EVEN_OFFSETS
    kv = o[7]
    sec = lambda start, width: np.arange(start, start + width)
    pad = lambda width: np.full((width,), -1)
    perm = np.concatenate([
        sec(o[0], 512), sec(o[3], 512), sec(o[6], 512),
        sec(o[1], 128), sec(kv + 2 * 128, 128), sec(kv + 4 * 128, 128), sec(o[4], 64), pad(64),
        sec(o[6], 512), sec(o[2], 128), sec(kv, 128), sec(kv + 128, 128),
        sec(kv + 3 * 128, 128), sec(kv + 5 * 128, 128), sec(o[5], 8), sec(o[8], 24), pad(96)])
    assert perm.shape[0] == EV_W
    w_ext = jnp.concatenate([w_in, jnp.zeros((w_in.shape[0], 1), w_in.dtype)], axis=1)
    return w_ext[:, np.where(perm < 0, w_in.shape[1], perm)].astype(BF16)


def _rope(x, pos):
    half = ROPE_DIM // 2
    inv = ROPE_THETA ** (-jnp.arange(half, dtype=F32) / half)
    ang = pos.astype(F32)[:, None] * inv[None, :]
    cos, sin = jnp.cos(ang)[:, None, :], jnp.sin(ang)[:, None, :]
    x1 = x[..., :half]
    x2 = x[..., half:ROPE_DIM]
    rot = jnp.concatenate([x1 * cos - x2 * sin, x1 * sin + x2 * cos], -1)
    return jnp.concatenate([rot, x[..., ROPE_DIM:]], -1)


def _masked_softmax(s, mask):
    s = jnp.where(mask, s, NEG)
    return jax.nn.softmax(s, axis=-1) * mask


def _shared_attend(q, k, v, mask):
    n, tq, h, d = q.shape
    g = k.shape[2]
    qg = q.reshape(n, tq, g, h // g, d)
    s = jnp.einsum('ntgrd,nsgd->ntgrs', qg, k, precision=HIGHEST) / math.sqrt(d)
    p = _masked_softmax(s, mask[None, :, None, None, :])
    o = jnp.einsum('ntgrs,nsgd->ntgrd', p, v, precision=HIGHEST)
    return o.reshape(n, tq, h, d), p


def _gathered_attend(q, k, v, valid):
    n, tq, h, d = q.shape
    g = k.shape[2]
    qg = q.reshape(n, tq, g, h // g, d)
    s = jnp.einsum('ntgrd,ntgsd->ntgrs', qg, k, precision=HIGHEST) / math.sqrt(d)
    p = _masked_softmax(s, valid[:, :, :, None, :])
    o = jnp.einsum('ntgrs,ntgsd->ntgrd', p, v, precision=HIGHEST)
    return o.reshape(n, tq, h, d)


def _gather_pages(pool, layer, page_table):
    g = pool[layer, page_table]
    return g.reshape((g.shape[0], g.shape[1] * g.shape[2]) + g.shape[3:])


def _even_project_sample(h, pos, w_in):
    n, t, _ = h.shape
    qa, ka, va, qi, ki, wi, qb, kvb, gb = jnp.split(jnp.matmul(h, w_in, precision=HIGHEST), EVEN_OFFSETS, axis=-1)
    qa = _rope(qa.reshape(n, t, A_HEADS, HEAD_DIM), pos)
    ka = _rope(ka.reshape(n, t, A_KV_HEADS, HEAD_DIM), pos)
    va = va.reshape(n, t, A_KV_HEADS, HEAD_DIM)
    qi = _rope(qi.reshape(n, t, IDX_HEADS, IDX_DIM), pos)
    ki = _rope(ki.reshape(n, t, 1, IDX_DIM), pos)[:, :, 0]
    qb = qb.reshape(n, t, B_HEADS, HEAD_DIM)
    kvb = kvb.reshape(n, t, 6, B_KV_HEADS, HEAD_DIM)
    kc, vc, vs, vw = kvb[:, :, 0], kvb[:, :, 1], kvb[:, :, 3], kvb[:, :, 5]
    ks = _rope(kvb[:, :, 2], pos)
    kw = _rope(kvb[:, :, 4], pos)
    gb = gb.reshape(n, t, B_HEADS, 3)
    return (qa, ka, va, qi, ki, wi), (qb, gb, kc, vc, ks, vs, kw, vw)


def _dsa_sample(qa, qi, wi, ka, va, ki, cache_k, cache_v, cache_kidx, layer, page_table, pos):
    n, tq = qa.shape[:2]
    past = page_table.shape[1] * PAGE_SIZE
    kidx_all = jnp.concatenate([_gather_pages(cache_kidx, layer, page_table), ki], axis=1)
    topk = min(IDX_TOPK, kidx_all.shape[1] // 4)
    s = jax.nn.relu(jnp.einsum('nthd,nsd->nths', qi, kidx_all, precision=HIGHEST))
    score = jnp.einsum('nth,nths->nts', wi, s, precision=HIGHEST)
    k_pos = jnp.arange(kidx_all.shape[1])
    score = jnp.where((k_pos[None, :] <= pos[:, None])[None], score, NEG)
    _, idx = lax.top_k(score, topk)
    valid = idx <= pos[None, :, None]
    bn = jnp.arange(n)[:, None, None]
    pidx = jnp.minimum(idx, past - 1)
    phys = page_table[bn, pidx // PAGE_SIZE]
    off = pidx % PAGE_SIZE
    nidx = jnp.clip(idx - past, 0, tq - 1)
    is_past = (idx < past)[..., None, None]
    kg = jnp.where(is_past, cache_k[layer, phys, off], ka[bn, nidx]).transpose(0, 1, 3, 2, 4)
    vg = jnp.where(is_past, cache_v[layer, phys, off], va[bn, nidx]).transpose(0, 1, 3, 2, 4)
    return _gathered_attend(qa, kg, vg, valid[:, :, None, :])


def _compress_sample(rows, pos_emb, w1, w2):
    n, l, g, d = rows.shape
    n_cmp = (l - CMP_LEN) // CMP_STRIDE + 1
    r = CMP_LEN // CMP_STRIDE
    chunks = rows[:, :(n_cmp + r - 1) * CMP_STRIDE].reshape(n, n_cmp + r - 1, CMP_STRIDE, g, d)
    w1c = w1.reshape(r, CMP_STRIDE, d, CMP_HID)
    h = jnp.matmul(pos_emb.reshape(-1), w1, precision=HIGHEST)
    for j in range(r):
        h = h + jnp.einsum('nclgd,ldh->ncgh', chunks[:, j:j + n_cmp], w1c[j], precision=HIGHEST)
    return jnp.matmul(jax.nn.silu(h), w2, precision=HIGHEST)


def _to_blocks(rows):
    n, l, g, d = rows.shape
    n_slc = -(-l // SLC_LEN)
    rows = jnp.pad(rows, ((0, 0), (0, n_slc * SLC_LEN - l), (0, 0), (0, 0)))
    return rows.reshape(n, n_slc, SLC_LEN, g, d)


def _cmp_to_slc(n_cmp, n_slc):
    i = np.arange(n_cmp)[:, None] * CMP_STRIDE
    j = np.arange(n_slc)[None, :] * SLC_LEN
    return jnp.asarray((i < j + SLC_LEN) & (i + CMP_LEN > j), dtype=F32)


def _nsa_attend_sample(q_rot, q_raw, gates, q_pos, kc, vc, ks_blk, vs_blk, kw, vw, kw_pos):
    n, tq, h, d = q_rot.shape
    g = kc.shape[2]
    n_cmp, n_slc = kc.shape[1], ks_blk.shape[1]
    cmp_end = jnp.arange(n_cmp) * CMP_STRIDE + (CMP_LEN - 1)
    o_cmp, p_cmp = _shared_attend(q_raw, kc, vc, cmp_end[None, :] <= q_pos[:, None])
    p_slc = jnp.einsum('ntgrc,cj->ntgj', p_cmp, _cmp_to_slc(n_cmp, n_slc), precision=HIGHEST)
    blk = jnp.arange(n_slc)[None, :]
    cur = (q_pos // SLC_LEN)[:, None]
    forced = (blk == 0) | (blk == cur) | (blk == cur - 1)
    avail = blk * SLC_LEN <= q_pos[:, None]
    score = jnp.where(avail[None, :, None, :], p_slc + FORCE_BONUS * forced[None, :, None, :], NEG)
    _, sel = lax.top_k(score, min(SLC_TOPN, n_slc))
    bn = jnp.arange(n)[:, None, None, None]
    gi = jnp.arange(g)[None, None, :, None]
    ks = ks_blk[bn, sel, :, gi].reshape(n, tq, g, -1, d)
    vs = vs_blk[bn, sel, :, gi].reshape(n, tq, g, -1, d)
    sel_pos = (sel[..., None] * SLC_LEN + jnp.arange(SLC_LEN)).reshape(n, tq, g, -1)
    o_slc = _gathered_attend(q_rot, ks, vs, sel_pos <= q_pos[None, :, None, None])
    wmask = ((kw_pos[None, :] <= q_pos[:, None]) & (kw_pos[None, :] > q_pos[:, None] - WINDOW)
             & (kw_pos[None, :] >= 0))
    o_win, _ = _shared_attend(q_rot, kw, vw, wmask)
    gt = jax.nn.sigmoid(gates)
    return gt[..., 0:1] * o_cmp + gt[..., 1:2] * o_slc + gt[..., 2:3] * o_win


def _nsa_sample(qb, gb, kc, vc, ks, vs, kw, vw, pos, cache_ck, cache_cv, cache_sk, cache_sv,
                win_k, win_v, layer, page_table, cmp_k, cmp_v):
    cat = lambda pool, new: jnp.concatenate([_gather_pages(pool, layer, page_table), new], axis=1)
    kcc = _compress_sample(cat(cache_ck, kc), *cmp_k)
    vcc = _compress_sample(cat(cache_cv, vc), *cmp_v)
    ks_blk, vs_blk = _to_blocks(cat(cache_sk, ks)), _to_blocks(cat(cache_sv, vs))
    w, tq = win_k.shape[1], qb.shape[1]
    kw_all = jnp.concatenate([win_k, kw], axis=1)
    vw_all = jnp.concatenate([win_v, vw], axis=1)
    kw_pos = pos[0] - w + jnp.arange(w + tq)
    o = _nsa_attend_sample(_rope(qb, pos), qb, gb, pos, kcc, vcc, ks_blk, vs_blk, kw_all, vw_all, kw_pos)
    return o, kw_all[:, -w:], vw_all[:, -w:]


def _sb_block(q, k, v, mask, log_carry):
    z = jnp.einsum('nthd,nshd->nhts', q, k, precision=HIGHEST) / math.sqrt(q.shape[-1])
    u = jnp.where(mask, jax.nn.log_sigmoid(-z), 0.0)
    later = lax.cumsum(u, axis=3, reverse=True) - u + log_carry[..., None]
    a = jnp.where(mask, jnp.exp(jax.nn.log_sigmoid(z) + later), 0.0)
    o = jnp.einsum('nhts,nshd->nthd', a, v, precision=HIGHEST)
    return o, log_carry + u.sum(-1)


def _sb_sample(q, k_new, v_new, cache_k, cache_v, layer, page_table, pos):
    n, tq, h, _ = q.shape
    o, lc = _sb_block(q, k_new, v_new, pos[None, :] < pos[:, None], jnp.zeros((n, h, tq), F32))

    def step(carry, phys):
        o_acc, lcar = carry
        ob, lcar = _sb_block(q, cache_k[layer, phys], cache_v[layer, phys],
                             jnp.ones((tq, PAGE_SIZE), bool), lcar)
        return (o_acc + ob, lcar), None

    (o, _), _ = lax.scan(step, (o, lc), page_table.T[::-1])
    return o


def _ln(x, g, b):
    mu = jnp.mean(x, -1, keepdims=True)
    var = jnp.mean(jnp.square(x - mu), -1, keepdims=True)
    return (x - mu) * lax.rsqrt(var + LN_EPS) * g + b


def _swiglu_sample(h, wg, wu, wd):
    mm = functools.partial(jnp.matmul, precision=HIGHEST)
    return mm(jax.nn.silu(mm(h, wg)) * mm(h, wu), wd)


def _moe_sample(h, w_router, b_router, wg, wu, wd):
    logits = jnp.matmul(h, w_router, precision=HIGHEST) + b_router
    top_val, top_idx = lax.top_k(logits, TOP_K)
    gate = jax.nn.softmax(top_val, axis=-1)
    dense_gate = jnp.einsum('ntk,ntke->nte', gate, jax.nn.one_hot(top_idx, N_EXPERTS, dtype=F32))
    y = jnp.zeros_like(h)
    for e in range(N_EXPERTS):
        y = y + dense_gate[..., e:e + 1] * _swiglu_sample(h, wg[e], wu[e], wd[e])
    return y


def kernel(x_prompt, x_sample, cache_a_k, cache_a_v, cache_a_kidx, cache_b_cmp_k, cache_b_cmp_v, cache_b_slc_k, cache_b_slc_v, state_b_win_k, state_b_win_v, cache_c_k, cache_c_v, page_table, c_prompt, c_sample, w_ada_mix, b_ada_mix, ln_mix_g, ln_mix_b, w_ada_ffn, b_ada_ffn, ln_ffn_g, ln_ffn_b, w_in_even, w_out_even, cmp_pos_k, cmp_w1_k, cmp_w2_k, cmp_pos_v, cmp_w1_v, cmp_w2_v, w_ffn_gate, w_ffn_up, w_ffn_down, w_in_odd, w_out_odd, w_router, b_router, w_moe_gate, w_moe_up, w_moe_down):
    n_p, t_p, d = x_prompt.shape
    n_s, t_s = x_sample.shape[:2]
    past = page_table.shape[1] * PAGE_SIZE
    pos_p = jnp.arange(t_p)
    pos_s = past + jnp.arange(t_s)
    m_p = n_p * t_p
    tm = 512

    n_c = n_p + n_s
    mp = -(-n_c // SUBLANES) * SUBLANES
    c_all = jnp.zeros((mp, d), F32).at[:n_p].set(c_prompt).at[n_p:n_c].set(c_sample)
    ada_mix = ada_all(c_all, w_ada_mix, b_ada_mix)
    ada_ffn = ada_all(c_all, w_ada_ffn, b_ada_ffn)

    def mods(ada, layer):
        sh, sc, gt = jnp.split(ada[layer], 3, axis=-1)
        pr = tuple(a[:n_p].reshape(n_p, 1, d) for a in (sh, sc, gt))
        sa = tuple(a[n_p:n_c].reshape(n_s, 1, d) for a in (sh, sc, gt))
        return pr, sa

    xp = x_prompt.reshape(m_p, d)
    xs = x_sample
    tabs = rope_tables(pos_p)

    (sh_p, sc_p, gt_p), (sh_s, sc_s, gt_s) = mods(ada_mix, 0)
    w_even = _even_weight_layout(w_in_even[0])
    p0 = mod_project(xp, sc_p, sh_p, w_even, tabs, EV_ROPE_W, t_p, tm, 256)
    col = lambda c, w: p0[:, c:c + w]
    kc_p, vc_p = col(C_KC, 128).reshape(n_p, t_p, 128), col(C_VC, 128).reshape(n_p, t_p, 128)
    kcc = compress_rows(kc_p, cmp_pos_k[0], cmp_w1_k[0], cmp_w2_k[0])
    vcc = compress_rows(vc_p, cmp_pos_v[0], cmp_w1_v[0], cmp_w2_v[0])
    oa = dsa_prompt(p0, n_p, t_p)
    ob = nsa_prompt(p0, kcc, vcc, n_p, t_p)
    w_out0 = w_out_even[0].astype(BF16)
    xp = outproj_ln(oa, 0, ob, 0, w_out0, xp, gt_p, ln_mix_g[0], ln_mix_b[0], t_p, tm)

    kv4 = lambda a: a.reshape(1, n_p, t_p, 2, HEAD_DIM)
    wlen = min(WINDOW, t_p)
    even_p = dict(
        a_k=kv4(col(C_KA, 128)), a_v=kv4(col(C_VA, 128)), a_kidx=col(C_KI, 64).reshape(1, n_p, t_p, IDX_DIM),
        cmp_k=kv4(kc_p), cmp_v=kv4(vc_p), slc_k=kv4(col(C_KS, 128)), slc_v=kv4(col(C_VS, 128)),
        win_k=kv4(col(C_KW, 128))[:, :, -wlen:], win_v=kv4(col(C_VW, 128))[:, :, -wlen:])

    hs = xs * (1 + sc_s) + sh_s
    (qa, ka, va, qi, ki, wi), (qb, gb, kc, vc, ks, vs, kw, vw) = _even_project_sample(hs, pos_s, w_in_even[0])
    oa_s = _dsa_sample(qa, qi, wi, ka, va, ki, cache_a_k, cache_a_v, cache_a_kidx, 0, page_table, pos_s)
    cmp_k = (cmp_pos_k[0], cmp_w1_k[0], cmp_w2_k[0])
    cmp_v = (cmp_pos_v[0], cmp_w1_v[0], cmp_w2_v[0])
    ob_s, wk, wv = _nsa_sample(qb, gb, kc, vc, ks, vs, kw, vw, pos_s, cache_b_cmp_k, cache_b_cmp_v,
                               cache_b_slc_k, cache_b_slc_v, state_b_win_k[0], state_b_win_v[0],
                               0, page_table, cmp_k, cmp_v)
    ms = jnp.matmul(jnp.concatenate([oa_s.reshape(n_s, t_s, -1), ob_s.reshape(n_s, t_s, -1)], -1),
                    w_out_even[0], precision=HIGHEST)
    xs = _ln(ALPHA * xs + gt_s * ms, ln_mix_g[0], ln_mix_b[0])
    even_s = (ka, va, ki, kc, vc, ks, vs, wk, wv)

    (sh_p, sc_p, gt_p), (sh_s, sc_s, gt_s) = mods(ada_ffn, 0)
    xp = ffn_ln(xp, sc_p, sh_p, gt_p, w_ffn_gate[0].astype(BF16), w_ffn_up[0].astype(BF16),
                w_ffn_down[0].astype(BF16), ln_ffn_g[0], ln_ffn_b[0], t_p, tm, D_FF // 2)
    hs = xs * (1 + sc_s) + sh_s
    xs = _ln(ALPHA * xs + gt_s * _swiglu_sample(hs, w_ffn_gate[0], w_ffn_up[0], w_ffn_down[0]),
             ln_ffn_g[0], ln_ffn_b[0])

    (sh_p, sc_p, gt_p), (sh_s, sc_s, gt_s) = mods(ada_mix, 1)
    ones_t = (jnp.ones((t_p, LANES), F32), jnp.zeros((t_p, LANES), F32), jnp.zeros((t_p, LANES), F32))
    p1 = mod_project(xp, sc_p, sh_p, w_in_odd[0].astype(BF16), ones_t, 0, t_p, tm, 512)
    osb = sb_prompt(p1, n_p, t_p)
    xp = outproj_ln(osb, 0, osb, 1, w_out_odd[0].astype(BF16), xp, gt_p, ln_mix_g[1], ln_mix_b[1], t_p, tm)
    hd = C_HEADS * HEAD_DIM
    c_k_p = p1[:, hd:2 * hd].reshape(1, n_p, t_p, C_HEADS, HEAD_DIM)
    c_v_p = p1[:, 2 * hd:].reshape(1, n_p, t_p, C_HEADS, HEAD_DIM)

    hs = xs * (1 + sc_s) + sh_s
    q_s, k_s, v_s = jnp.split(jnp.matmul(hs, w_in_odd[0], precision=HIGHEST), 3, axis=-1)
    shp = (n_s, t_s, C_HEADS, HEAD_DIM)
    q_s, k_s, v_s = q_s.reshape(shp), k_s.reshape(shp), v_s.reshape(shp)
    ms = jnp.matmul(_sb_sample(q_s, k_s, v_s, cache_c_k, cache_c_v, 0, page_table, pos_s).reshape(n_s, t_s, -1),
                    w_out_odd[0], precision=HIGHEST)
    xs = _ln(ALPHA * xs + gt_s * ms, ln_mix_g[1], ln_mix_b[1])

    (sh_p, sc_p, gt_p), (sh_s, sc_s, gt_s) = mods(ada_ffn, 1)
    h_p, r_p = router(xp, sc_p, sh_p, w_router[0], b_router[0], t_p, tm)
    tok_of_slot, tile_expert, n_used, slot0, slot1 = moe_routing_tables(r_p)
    ys = moe_expert_rows(h_p, tok_of_slot, tile_expert, n_used, w_moe_gate[0].astype(BF16),
                         w_moe_up[0].astype(BF16), w_moe_down[0].astype(BF16))
    xp = moe_combine_ln(ys, slot0, slot1, r_p, xp, gt_p, ln_ffn_g[1], ln_ffn_b[1], t_p)
    hs = xs * (1 + sc_s) + sh_s
    xs = _ln(ALPHA * xs + gt_s * _moe_sample(hs, w_router[0], b_router[0], w_moe_gate[0], w_moe_up[0], w_moe_down[0]),
             ln_ffn_g[1], ln_ffn_b[1])

    st = lambda a: a[None]
    (a_k_s, a_v_s, a_kidx_s, b_cmp_k_s, b_cmp_v_s, b_slc_k_s, b_slc_v_s, b_win_k_s, b_win_v_s) = [st(a) for a in even_s]
    return (xp.reshape(n_p, t_p, d), xs,
            even_p['a_k'], a_k_s, even_p['a_v'], a_v_s, even_p['a_kidx'], a_kidx_s,
            even_p['cmp_k'], b_cmp_k_s, even_p['cmp_v'], b_cmp_v_s,
            even_p['slc_k'], b_slc_k_s, even_p['slc_v'], b_slc_v_s,
            even_p['win_k'], b_win_k_s, even_p['win_v'], b_win_v_s,
            c_k_p, st(k_s), c_v_p, st(v_s))
```

```python
import functools
import math

import numpy as np
import jax
import jax.numpy as jnp
from jax import lax
from jax.experimental import pallas as pl
from jax.experimental.pallas import tpu as pltpu

D_MODEL = 1024
PAGE_SIZE = 128
HEAD_DIM = 64
ROPE_DIM = HEAD_DIM // 4
ROPE_THETA = 500000.0
Q_BLOCK = 128
A_HEADS = 8
A_KV_HEADS = 2
IDX_HEADS = 8
IDX_DIM = 64
IDX_TOPK = 256
B_HEADS = 8
B_KV_HEADS = 2
CMP_LEN = 32
CMP_STRIDE = 16
CMP_HID = 128
SLC_LEN = 64
SLC_TOPN = 16
WINDOW = 512
FORCE_BONUS = 100.0
C_HEADS = 16
D_FF = 2816
N_EXPERTS = 8
TOP_K = 2
D_FF_EXPERT = 3584
DEPTH = 2
ALPHA = (2 * DEPTH) ** 0.25
LN_EPS = 1e-5
NEG = -1e30
EVEN_SPLIT = (A_HEADS * HEAD_DIM, A_KV_HEADS * HEAD_DIM, A_KV_HEADS * HEAD_DIM,
              IDX_HEADS * IDX_DIM, IDX_DIM, IDX_HEADS,
              B_HEADS * HEAD_DIM, 6 * B_KV_HEADS * HEAD_DIM, 3 * B_HEADS)
EVEN_OFFSETS = tuple(int(v) for v in np.cumsum(EVEN_SPLIT)[:-1])

LANES = 128
SUBLANES = 8
VMEM_LIMIT_BYTES = 56 * 1024 * 1024

F32 = jnp.float32
BF16 = jnp.bfloat16
I32 = jnp.int32
INT_MIN = -2 ** 31
HIGHEST = lax.Precision.HIGHEST

C_QA, C_QI, C_QBR, C_KA, C_KS, C_KW, C_KI = 0, 512, 1024, 1536, 1664, 1792, 1920
EV_ROPE_W = 2048
C_QB, C_VA, C_KC, C_VC, C_VS, C_VW, C_WG = 2048, 2560, 2688, 2816, 2944, 3072, 3200
EV_W = 3328
WG_GB_LANE = IDX_HEADS


def _cparams(sem):
    return pltpu.CompilerParams(dimension_semantics=sem, vmem_limit_bytes=VMEM_LIMIT_BYTES)


def _sortable_key(x):
    b = lax.bitcast_convert_type(x + 0.0, I32)
    return jnp.where(b < 0, b ^ jnp.int32(0x7FFFFFFF), b)


_KEY_NEG = int(np.array(NEG, np.float32).view(np.int32)) ^ 0x7FFFFFFF


def _layer_norm_rows(y, g, b):
    mu = jnp.mean(y, axis=-1, keepdims=True)
    d = y - mu
    var = jnp.mean(d * d, axis=-1, keepdims=True)
    return d * lax.rsqrt(var + LN_EPS) * g + b


def _silu(x):
    return x * jax.nn.sigmoid(x)


def _dot_t(a, b):
    return lax.dot_general(a, b, (((1,), (1,)), ((), ())), preferred_element_type=F32)


def _dot(a, b):
    return jnp.dot(a, b, preferred_element_type=F32)


def _topk_bounds(load, n_dyn, n_total, tail_key, k, rows, cw, idx_bits):
    lane = lax.broadcasted_iota(I32, (rows, cw), 1)
    n_tail = ((n_total - n_dyn) * cw).astype(F32) if not isinstance(n_dyn, int) else float((n_total - n_dyn) * cw)
    dyn_w = n_dyn * cw
    kf = float(k)

    def count(pred):
        def body(c, tot):
            hit = pred(load(c), lane + c * cw).astype(F32)
            for t in range(cw // LANES):
                tot = tot + hit[:, t * LANES:(t + 1) * LANES]
            return tot
        tot = lax.fori_loop(0, n_dyn, body, jnp.zeros((rows, LANES), F32))
        return jnp.sum(tot, axis=1, keepdims=True)

    def bit_body(it, u):
        bit = lax.shift_left(jnp.int32(1), jnp.int32(31) - it)
        cu = u | bit
        cand = cu ^ jnp.int32(INT_MIN)
        cnt = count(lambda keys, idx: keys >= cand) + n_tail * (jnp.int32(tail_key) >= cand).astype(F32)
        return jnp.where(cnt >= kf, cu, u)

    u = lax.fori_loop(0, 32, bit_body, jnp.zeros((rows, 1), I32))
    thr = u ^ jnp.int32(INT_MIN)
    tail_gt = (jnp.int32(tail_key) > thr).astype(F32)
    tail_eq = (jnp.int32(tail_key) == thr).astype(F32)
    c_gt = count(lambda keys, idx: keys > thr) + n_tail * tail_gt
    need = kf - c_gt

    def j_body(it, j0):
        bit = lax.shift_left(jnp.int32(1), jnp.int32(idx_bits - 1) - it)
        cj = j0 | bit
        f = count(lambda keys, idx: (keys == thr) & (idx < cj))
        f = f + tail_eq * jnp.clip((cj - dyn_w).astype(F32), 0.0, n_tail)
        return jnp.where(f < need, cj, j0)

    c_eq = count(lambda keys, idx: keys == thr) + n_tail * tail_eq
    j0 = lax.cond(jnp.max(c_eq - need) > 0.0,
                  lambda: lax.fori_loop(0, idx_bits, j_body, jnp.zeros((rows, 1), I32)),
                  lambda: jnp.full((rows, 1), 2 ** idx_bits - 1, I32))
    return thr, j0


def _ada_kernel(c_ref, w_ref, b_ref, o_ref):
    s = _silu(c_ref[...])
    o_ref[0] = jnp.dot(s, w_ref[0], precision=HIGHEST, preferred_element_type=F32) + b_ref[0]


def ada_all(c_all, w, b):
    mp, d = c_all.shape
    nl, _, n3 = w.shape
    tn = 512
    return pl.pallas_call(
        _ada_kernel,
        grid=(nl, n3 // tn),
        in_specs=[pl.BlockSpec((mp, d), lambda l, j: (0, 0)),
                  pl.BlockSpec((1, d, tn), lambda l, j: (l, 0, j)),
                  pl.BlockSpec((1, 1, tn), lambda l, j: (l, 0, j))],
        out_specs=pl.BlockSpec((1, mp, tn), lambda l, j: (l, 0, j)),
        out_shape=jax.ShapeDtypeStruct((nl, mp, n3), F32),
        compiler_params=_cparams(("arbitrary", "arbitrary")),
        name="ada",
    )(c_all, w, b.reshape(nl, 1, n3))


def _proj_kernel(x_ref, sc_ref, sh_ref, w_ref, cos_ref, sn_ref, sp_ref, o_ref, *, n_rope_tiles, tn):
    j = pl.program_id(1)
    h = (x_ref[...] * (1.0 + sc_ref[0]) + sh_ref[0]).astype(BF16)
    acc = _dot(h, w_ref[...])

    @pl.when(j < n_rope_tiles)
    def _():
        c, sn, sp = cos_ref[...], sn_ref[...], sp_ref[...]
        for k in range(tn // LANES):
            seg = acc[:, k * LANES:(k + 1) * LANES]
            o_ref[:, k * LANES:(k + 1) * LANES] = (
                seg * c + pltpu.roll(seg, LANES - ROPE_DIM // 2, 1) * sn + pltpu.roll(seg, ROPE_DIM // 2, 1) * sp)

    @pl.when(j >= n_rope_tiles)
    def _():
        o_ref[...] = acc


def mod_project(x, sc, sh, w, rope_tabs, rope_width, seq, tm, tn):
    m, d = x.shape
    n = w.shape[1]
    cos_t, sn_t, sp_t = rope_tabs
    tiles_per_seq = seq // tm
    mod_spec = pl.BlockSpec((1, sc.shape[1], d), lambda i, j: (i // tiles_per_seq, 0, 0))
    tab_spec = pl.BlockSpec((tm, LANES), lambda i, j: (i % tiles_per_seq, 0))
    return pl.pallas_call(
        functools.partial(_proj_kernel, n_rope_tiles=rope_width // tn, tn=tn),
        grid=(m // tm, n // tn),
        in_specs=[pl.BlockSpec((tm, d), lambda i, j: (i, 0)), mod_spec, mod_spec,
                  pl.BlockSpec((d, tn), lambda i, j: (0, j)), tab_spec, tab_spec, tab_spec],
        out_specs=pl.BlockSpec((tm, tn), lambda i, j: (i, j)),
        out_shape=jax.ShapeDtypeStruct((m, n), F32),
        compiler_params=_cparams(("parallel", "arbitrary")),
        name="mod_project",
    )(x, sc, sh, w, cos_t, sn_t, sp_t)


def _outproj_ln_kernel(a1_ref, a2_ref, w_ref, x_ref, gt_ref, g_ref, b_ref, o_ref):
    a = jnp.concatenate([a1_ref[...], a2_ref[...]], axis=1).astype(BF16)
    y = ALPHA * x_ref[...] + gt_ref[0] * _dot(a, w_ref[...])
    o_ref[...] = _layer_norm_rows(y, g_ref[...], b_ref[...])


def outproj_ln(a1, a1_blk, a2, a2_blk, w, x, gt, ln_g, ln_b, seq, tm):
    m, d = x.shape
    half = d // 2
    tiles_per_seq = seq // tm
    return pl.pallas_call(
        _outproj_ln_kernel,
        grid=(m // tm,),
        in_specs=[pl.BlockSpec((tm, half), lambda i: (i, a1_blk)),
                  pl.BlockSpec((tm, half), lambda i: (i, a2_blk)),
                  pl.BlockSpec((d, d), lambda i: (0, 0)),
                  pl.BlockSpec((tm, d), lambda i: (i, 0)),
                  pl.BlockSpec((1, gt.shape[1], d), lambda i: (i // tiles_per_seq, 0, 0)),
                  pl.BlockSpec((1, d), lambda i: (0, 0)),
                  pl.BlockSpec((1, d), lambda i: (0, 0))],
        out_specs=pl.BlockSpec((tm, d), lambda i: (i, 0)),
        out_shape=jax.ShapeDtypeStruct((m, d), F32),
        compiler_params=_cparams(("parallel",)),
        name="outproj_ln",
    )(a1, a2, w, x, gt, ln_g.reshape(1, d), ln_b.reshape(1, d))


def _ffn_ln_kernel(x_ref, sc_ref, sh_ref, gt_ref, wg_ref, wu_ref, wd_ref, g_ref, b_ref, o_ref, h_scr, acc_scr):
    j = pl.program_id(1)

    @pl.when(j == 0)
    def _():
        h_scr[...] = (x_ref[...] * (1.0 + sc_ref[0]) + sh_ref[0]).astype(BF16)
        acc_scr[...] = jnp.zeros_like(acc_scr)

    h = h_scr[...]
    a = (_silu(_dot(h, wg_ref[...])) * _dot(h, wu_ref[...])).astype(BF16)
    acc_scr[...] += _dot(a, wd_ref[...])

    @pl.when(j == pl.num_programs(1) - 1)
    def _():
        y = ALPHA * x_ref[...] + gt_ref[0] * acc_scr[...]
        o_ref[...] = _layer_norm_rows(y, g_ref[...], b_ref[...])


def ffn_ln(x, sc, sh, gt, wg, wu, wd, ln_g, ln_b, seq, tm, tf):
    m, d = x.shape
    f = wg.shape[1]
    tiles_per_seq = seq // tm
    mod_spec = pl.BlockSpec((1, sc.shape[1], d), lambda i, j: (i // tiles_per_seq, 0, 0))
    vec_spec = pl.BlockSpec((1, d), lambda i, j: (0, 0))
    return pl.pallas_call(
        _ffn_ln_kernel,
        grid=(m // tm, f // tf),
        in_specs=[pl.BlockSpec((tm, d), lambda i, j: (i, 0)), mod_spec, mod_spec, mod_spec,
                  pl.BlockSpec((d, tf), lambda i, j: (0, j)),
                  pl.BlockSpec((d, tf), lambda i, j: (0, j)),
                  pl.BlockSpec((tf, d), lambda i, j: (j, 0)), vec_spec, vec_spec],
        out_specs=pl.BlockSpec((tm, d), lambda i, j: (i, 0)),
        out_shape=jax.ShapeDtypeStruct((m, d), F32),
        scratch_shapes=[pltpu.VMEM((tm, d), BF16), pltpu.VMEM((tm, d), F32)],
        compiler_params=_cparams(("parallel", "arbitrary")),
        name="ffn_ln",
    )(x, sc, sh, gt, wg, wu, wd, ln_g.reshape(1, d), ln_b.reshape(1, d))


def _head_tile(x, h, dst):
    t = x[:, (h // 2) * LANES:(h // 2 + 1) * LANES]
    if h % 2 != dst:
        t = pltpu.roll(t, HEAD_DIM, 1)
    half = lax.broadcasted_iota(I32, t.shape, 1) // HEAD_DIM
    return jnp.where(half == dst, t, 0.0)


def _stack_group(x, g, heads_per_group, scale):
    tiles = [_head_tile(x, g * heads_per_group + r, g) * scale for r in range(heads_per_group)]
    return jnp.concatenate(tiles, axis=0).astype(BF16)


def _unstack_groups(o_groups, heads_per_group, q):
    n_heads = len(o_groups) * heads_per_group
    lane_half = lax.broadcasted_iota(I32, (q, LANES), 1) // HEAD_DIM
    pairs = []
    for hp in range(n_heads // 2):
        halves = []
        for pos in range(2):
            h = 2 * hp + pos
            g, r = divmod(h, heads_per_group)
            t = o_groups[g][r * q:(r + 1) * q]
            if g != pos:
                t = pltpu.roll(t, HEAD_DIM, 1)
            halves.append(t)
        pairs.append(jnp.where(lane_half == 0, halves[0], halves[1]))
    return jnp.concatenate(pairs, axis=1)


def _flash_step(qg, kc, vc, mk, carry):
    m, l, acc = carry
    s = _dot_t(qg, kc)
    s = jnp.where(mk > 0.5, s, NEG)
    m_new = jnp.maximum(m, jnp.max(s, axis=1, keepdims=True))
    alpha = jnp.exp(m - m_new)
    p = jnp.exp(s - m_new) * mk
    l = alpha * l + jnp.sum(p, axis=1, keepdims=True)
    acc = alpha * acc + _dot(p.astype(BF16), vc)
    return m_new, l, acc


def _flash_init(rows):
    return (jnp.full((rows, 1), NEG, F32), jnp.zeros((rows, 1), F32), jnp.zeros((rows, LANES), F32))


def _flash_out(carry):
    _, l, acc = carry
    return acc / jnp.maximum(l, 1e-30)


DSA_CW = 512


def _dsa_kernel(qa_ref, qi_ref, wg_ref, ki_ref, ka_ref, va_ref, o_ref, key_scr, msk_scr, *, seq, topk):
    q = Q_BLOCK
    i = pl.program_id(1)
    q0 = i * q
    nch = seq // DSA_CW
    n_dyn = (q0 + q + DSA_CW - 1) // DSA_CW
    row = q0 + lax.broadcasted_iota(I32, (q, 1), 0)
    lane = lax.broadcasted_iota(I32, (q, DSA_CW), 1)

    qi = qi_ref[...]
    wi = wg_ref[...]
    qih = []
    for h in range(IDX_HEADS):
        t = qi[:, (h // 2) * LANES:(h // 2 + 1) * LANES]
        if h % 2:
            t = pltpu.roll(t, HEAD_DIM, 1)
        qih.append(t.astype(BF16))

    def score_body(c, _):
        off = pl.multiple_of(c * DSA_CW, DSA_CW)
        kc = ki_ref[pl.ds(off, DSA_CW), :].astype(BF16)
        acc = jnp.zeros((q, DSA_CW), F32)
        for h in range(IDX_HEADS):
            acc = acc + wi[:, h:h + 1] * jnp.maximum(_dot_t(qih[h], kc), 0.0)
        acc = jnp.where(lane + off <= row, acc, NEG)
        key_scr[c] = _sortable_key(acc)
        return 0

    lax.fori_loop(0, n_dyn, score_body, 0)

    thr, j0 = _topk_bounds(lambda c: key_scr[c], n_dyn, nch, _KEY_NEG, topk, q, DSA_CW,
                           int(math.log2(seq)))

    def mask_body(c, _):
        keys = key_scr[c]
        idx = lane + c * DSA_CW
        sel = (keys > thr) | ((keys == thr) & (idx <= j0))
        msk_scr[c] = (sel & (idx <= row)).astype(F32)
        return 0

    lax.fori_loop(0, n_dyn, mask_body, 0)

    qa = qa_ref[...]
    hpg = A_HEADS // A_KV_HEADS
    outs = []
    for g in range(A_KV_HEADS):
        qg = _stack_group(qa, g, hpg, 1.0 / math.sqrt(HEAD_DIM))

        def attn_body(c, carry, qg=qg):
            off = pl.multiple_of(c * DSA_CW, DSA_CW)
            kc = ka_ref[pl.ds(off, DSA_CW), :].astype(BF16)
            vc = va_ref[pl.ds(off, DSA_CW), :].astype(BF16)
            mk = msk_scr[c]
            mk = jnp.concatenate([mk] * hpg, axis=0)
            return _flash_step(qg, kc, vc, mk, carry)

        outs.append(_flash_out(lax.fori_loop(0, n_dyn, attn_body, _flash_init(hpg * q))))
    o_ref[...] = _unstack_groups(outs, hpg, q)


def dsa_prompt(p, n_seq, seq):
    nq = seq // Q_BLOCK
    topk = min(IDX_TOPK, seq // 4)
    qspec = lambda cb: pl.BlockSpec((Q_BLOCK, 512), lambda n, i: (n * nq + i, cb))
    kvspec = lambda cb: pl.BlockSpec((seq, LANES), lambda n, i: (n, cb))
    return pl.pallas_call(
        functools.partial(_dsa_kernel, seq=seq, topk=topk),
        grid=(n_seq, nq),
        in_specs=[qspec(C_QA // 512), qspec(C_QI // 512),
                  pl.BlockSpec((Q_BLOCK, LANES), lambda n, i: (n * nq + i, C_WG // LANES)),
                  kvspec(C_KI // LANES), kvspec(C_KA // LANES), kvspec(C_VA // LANES)],
        out_specs=pl.BlockSpec((Q_BLOCK, 512), lambda n, i: (n * nq + i, 0)),
        out_shape=jax.ShapeDtypeStruct((n_seq * seq, 512), F32),
        scratch_shapes=[pltpu.VMEM((seq // DSA_CW, Q_BLOCK, DSA_CW), I32),
                        pltpu.VMEM((seq // DSA_CW, Q_BLOCK, DSA_CW), F32)],
        compiler_params=_cparams(("parallel", "arbitrary")),
        name="dsa_prompt",
    )(p, p, p, p, p, p)


def _compress_kernel(x_ref, pa_ref, pb_ref, w1a_ref, w1b_ref, w2_ref, o_ref):
    x = x_ref[0]
    nchunk = x.shape[0]
    a = _dot((x + pa_ref[...]).astype(BF16), w1a_ref[...])
    b = _dot((x + pb_ref[...]).astype(BF16), w1b_ref[...])
    h = a + pltpu.roll(b, nchunk - 1, 0)
    o_ref[0] = _dot(_silu(h).astype(BF16), w2_ref[...])


def compress_rows(rows, pos_emb, w1, w2):
    n, l, gd = rows.shape
    g = gd // HEAD_DIM
    nchunk = l // CMP_STRIDE
    x = rows.reshape(n, nchunk, CMP_STRIDE * gd)
    r = CMP_LEN // CMP_STRIDE
    w1c = w1.reshape(r, CMP_STRIDE, HEAD_DIM, CMP_HID)
    eye = jnp.eye(g, dtype=F32)
    big = [jnp.einsum('ldh,gk->lgdkh', w1c[j], eye).reshape(CMP_STRIDE * gd, g * CMP_HID).astype(BF16) for j in range(r)]
    w2big = jnp.einsum('hd,gk->ghkd', w2, eye).reshape(g * CMP_HID, gd).astype(BF16)
    pe = pos_emb.reshape(r, CMP_STRIDE, 1, HEAD_DIM)
    pab = [jnp.broadcast_to(pe[j], (CMP_STRIDE, g, HEAD_DIM)).reshape(1, CMP_STRIDE * gd) for j in range(r)]
    k2 = CMP_STRIDE * gd
    return pl.pallas_call(
        _compress_kernel,
        grid=(n,),
        in_specs=[pl.BlockSpec((1, nchunk, k2), lambda i: (i, 0, 0)),
                  pl.BlockSpec((1, k2), lambda i: (0, 0)), pl.BlockSpec((1, k2), lambda i: (0, 0)),
                  pl.BlockSpec((k2, g * CMP_HID), lambda i: (0, 0)),
                  pl.BlockSpec((k2, g * CMP_HID), lambda i: (0, 0)),
                  pl.BlockSpec((g * CMP_HID, gd), lambda i: (0, 0))],
        out_specs=pl.BlockSpec((1, nchunk, gd), lambda i: (i, 0, 0)),
        out_shape=jax.ShapeDtypeStruct((n, nchunk, gd), F32),
        compiler_params=_cparams(("parallel",)),
        name="compress",
    )(x, pab[0], pab[1], big[0], big[1], w2big)


NSA_CW = 512


def _nsa_kernel(qb_ref, qr_ref, wg_ref, kcc_ref, vcc_ref, ks_ref, vs_ref, kw_ref, vw_ref, o_ref, *, seq):
    q = Q_BLOCK
    i = pl.program_id(1)
    q0 = i * q
    hpg = B_HEADS // B_KV_HEADS
    ncmp = seq // CMP_STRIDE
    nslc = seq // SLC_LEN
    n_dyn = (q0 + q + NSA_CW - 1) // NSA_CW
    scale = 1.0 / math.sqrt(HEAD_DIM)
    row = q0 + lax.broadcasted_iota(I32, (q, 1), 0)

    qb = qb_ref[...]
    qr = qr_ref[...]
    kcc = kcc_ref[0].astype(BF16)
    vcc = vcc_ref[0].astype(BF16)

    cidx = lax.broadcasted_iota(I32, (q, ncmp), 1)
    cmask = (cidx * CMP_STRIDE + (CMP_LEN - 1) <= row).astype(F32)
    cmask_g = jnp.concatenate([cmask] * hpg, axis=0)
    assert nslc <= LANES
    ci = lax.broadcasted_iota(I32, (ncmp, LANES), 0) * CMP_STRIDE
    sj = lax.broadcasted_iota(I32, (ncmp, LANES), 1) * SLC_LEN
    cmp2slc = ((ci < sj + SLC_LEN) & (ci + CMP_LEN > sj) & (sj < seq)).astype(BF16)
    blk = lax.broadcasted_iota(I32, (q, LANES), 1)
    cur = row // SLC_LEN
    forced = ((blk == 0) | (blk == cur) | (blk == cur - 1)).astype(F32)
    avail = (blk * SLC_LEN <= row) & (blk < nslc)

    o_cmp, sel_keys = [], []
    for g in range(B_KV_HEADS):
        qg = _stack_group(qb, g, hpg, scale)
        s = jnp.where(cmask_g > 0.5, _dot_t(qg, kcc), NEG)
        e = jnp.exp(s - jnp.max(s, axis=1, keepdims=True)) * cmask_g
        p = e / jnp.maximum(jnp.sum(e, axis=1, keepdims=True), 1e-30)
        o_cmp.append(_dot(p.astype(BF16), vcc))
        psum = p[0:q]
        for r in range(1, hpg):
            psum = psum + p[r * q:(r + 1) * q]
        p_hi = psum.astype(BF16)
        p_lo = (psum - p_hi.astype(F32)).astype(BF16)
        p_slc = _dot(p_hi, cmp2slc) + _dot(p_lo, cmp2slc)
        score = jnp.where(avail, p_slc + FORCE_BONUS * forced, NEG)
        sel_keys.append(jnp.where(blk < nslc, _sortable_key(score), jnp.int32(INT_MIN)))

    keys = jnp.concatenate(sel_keys, axis=0)
    ntop = min(SLC_TOPN, nslc)
    thr, j0 = _topk_bounds(lambda c: keys, 1, 1, INT_MIN, ntop, B_KV_HEADS * q, LANES, int(math.log2(LANES)))
    lane_b = lax.broadcasted_iota(I32, keys.shape, 1)
    sel = ((keys > thr) | ((keys == thr) & (lane_b <= j0))).astype(BF16)

    lane = lax.broadcasted_iota(I32, (q, NSA_CW), 1)
    eb = lax.broadcasted_iota(I32, (LANES, NSA_CW), 0)
    es = lax.broadcasted_iota(I32, (LANES, NSA_CW), 1)

    o_slc, o_win = [], []
    for g in range(B_KV_HEADS):
        qg = _stack_group(qr, g, hpg, scale)
        bm = sel[g * q:(g + 1) * q]

        def slc_body(c, carry, qg=qg, bm=bm):
            off = pl.multiple_of(c * NSA_CW, NSA_CW)
            kc = ks_ref[pl.ds(off, NSA_CW), :].astype(BF16)
            vc = vs_ref[pl.ds(off, NSA_CW), :].astype(BF16)
            expand = ((es + off) // SLC_LEN == eb).astype(BF16)
            mk = _dot(bm, expand) * (lane + off <= row).astype(F32)
            mk = jnp.concatenate([mk] * hpg, axis=0)
            return _flash_step(qg, kc, vc, mk, carry)

        o_slc.append(_flash_out(lax.fori_loop(0, n_dyn, slc_body, _flash_init(hpg * q))))

        lane_w = lax.broadcasted_iota(I32, (q, q), 1)

        def win_body(kb, carry, qg=qg):
            off = pl.multiple_of(kb * q, q)
            kc = kw_ref[pl.ds(off, q), :].astype(BF16)
            vc = vw_ref[pl.ds(off, q), :].astype(BF16)
            col = lane_w + off
            mk = ((col <= row) & (col > row - WINDOW)).astype(F32)
            mk = jnp.concatenate([mk] * hpg, axis=0)
            return _flash_step(qg, kc, vc, mk, carry)

        kb_lo = jnp.maximum(i - WINDOW // q, 0)
        o_win.append(_flash_out(lax.fori_loop(kb_lo, i + 1, win_body, _flash_init(hpg * q))))

    gate = jax.nn.sigmoid(wg_ref[...])
    outs = []
    for g in range(B_KV_HEADS):
        cols = [[gate[:, WG_GB_LANE + (g * hpg + r) * 3 + b: WG_GB_LANE + (g * hpg + r) * 3 + b + 1]
                 for r in range(hpg)] for b in range(3)]
        gc = [jnp.concatenate(cols[b], axis=0) for b in range(3)]
        outs.append(gc[0] * o_cmp[g] + gc[1] * o_slc[g] + gc[2] * o_win[g])
    o_ref[...] = _unstack_groups(outs, hpg, q)


def nsa_prompt(p, kcc, vcc, n_seq, seq):
    nq = seq // Q_BLOCK
    ncmp = seq // CMP_STRIDE
    qspec = lambda cb: pl.BlockSpec((Q_BLOCK, 512), lambda n, i: (n * nq + i, cb))
    kvspec = lambda cb: pl.BlockSpec((seq, LANES), lambda n, i: (n, cb))
    cspec = pl.BlockSpec((1, ncmp, LANES), lambda n, i: (n, 0, 0))
    return pl.pallas_call(
        functools.partial(_nsa_kernel, seq=seq),
        grid=(n_seq, nq),
        in_specs=[qspec(C_QB // 512), qspec(C_QBR // 512),
                  pl.BlockSpec((Q_BLOCK, LANES), lambda n, i: (n * nq + i, C_WG // LANES)),
                  cspec, cspec,
                  kvspec(C_KS // LANES), kvspec(C_VS // LANES), kvspec(C_KW // LANES), kvspec(C_VW // LANES)],
        out_specs=pl.BlockSpec((Q_BLOCK, 512), lambda n, i: (n * nq + i, 0)),
        out_shape=jax.ShapeDtypeStruct((n_seq * seq, 512), F32),
        compiler_params=_cparams(("parallel", "arbitrary")),
        name="nsa_prompt",
    )(p, p, p, kcc, vcc, p, p, p, p)


SB_DEAD = -110.0
SB_PAIRS = 4


def _log_sigmoid_pair(z):
    ls = jnp.minimum(z, 0.0) - jnp.log(1.0 + jnp.exp(-jnp.abs(z)))
    return ls, ls - z


def _split_bf16(x):
    hi = x.astype(BF16)
    return hi, (x - hi.astype(F32)).astype(BF16)


def _sb_kernel(q_ref, k_ref, v_ref, o_ref):
    q = Q_BLOCK
    i = pl.program_id(2)
    q0 = i * q
    row = q0 + lax.broadcasted_iota(I32, (q, 1), 0)
    lane = lax.broadcasted_iota(I32, (q, q), 1)
    tri = (lax.broadcasted_iota(I32, (q, q), 0) > lane).astype(BF16)
    half = lax.broadcasted_iota(I32, (q, LANES), 1) // HEAD_DIM
    qq = q_ref[...] * (1.0 / math.sqrt(HEAD_DIM))
    qsplit = [[_split_bf16(jnp.where(half == h, qq[:, hp * LANES:(hp + 1) * LANES], 0.0)) for h in range(2)]
              for hp in range(SB_PAIRS)]
    nh = 2 * SB_PAIRS

    def cond(st):
        j, carries = st[0], st[1:1 + nh]
        top = carries[0]
        for c in carries[1:]:
            top = jnp.maximum(top, c)
        return (j >= 0) & (jnp.max(top) > SB_DEAD)

    def body(st):
        j, carries, accs = st[0], list(st[1:1 + nh]), list(st[1 + nh:])
        off = pl.multiple_of(j * q, q)
        mask = lane + off < row
        for hp in range(SB_PAIRS):
            k_hi, k_lo = _split_bf16(k_ref[pl.ds(off, q), hp * LANES:(hp + 1) * LANES])
            vb = v_ref[pl.ds(off, q), hp * LANES:(hp + 1) * LANES].astype(BF16)
            for h in range(2):
                q_hi, q_lo = qsplit[hp][h]
                z = _dot_t(q_hi, k_hi) + _dot_t(q_hi, k_lo) + _dot_t(q_lo, k_hi)
                ls, lsn = _log_sigmoid_pair(z)
                u = jnp.where(mask, lsn, 0.0)
                u_hi, u_lo = _split_bf16(u)
                later = _dot(u_hi, tri) + _dot(u_lo, tri) + carries[2 * hp + h]
                a = jnp.where(mask, jnp.exp(ls + later), 0.0)
                accs[hp] = accs[hp] + jnp.where(half == h, _dot(a.astype(BF16), vb), 0.0)
                carries[2 * hp + h] = carries[2 * hp + h] + jnp.sum(u, axis=1, keepdims=True)
        return (j - 1, *carries, *accs)

    z1 = jnp.zeros((q, 1), F32)
    zl = jnp.zeros((q, LANES), F32)
    st = lax.while_loop(cond, body, (i,) + (z1,) * nh + (zl,) * SB_PAIRS)
    o_ref[...] = jnp.concatenate(st[1 + nh:], axis=1)


def sb_prompt(p1, n_seq, seq):
    nq = seq // Q_BLOCK
    width = SB_PAIRS * LANES
    ngrp = C_HEADS * HEAD_DIM // width
    return pl.pallas_call(
        _sb_kernel,
        grid=(n_seq, ngrp, nq),
        in_specs=[pl.BlockSpec((Q_BLOCK, width), lambda n, hg, i: (n * nq + i, hg)),
                  pl.BlockSpec((seq, width), lambda n, hg, i: (n, ngrp + hg)),
                  pl.BlockSpec((seq, width), lambda n, hg, i: (n, 2 * ngrp + hg))],
        out_specs=pl.BlockSpec((Q_BLOCK, width), lambda n, hg, i: (n * nq + i, hg)),
        out_shape=jax.ShapeDtypeStruct((n_seq * seq, C_HEADS * HEAD_DIM), F32),
        compiler_params=_cparams(("parallel", "parallel", "arbitrary")),
        name="sb_prompt",
    )(p1, p1, p1)


def _sb_sample_kernel(pt_ref, q_ref, ck_hbm, cv_hbm, o_ref, kbuf, vbuf, sem, *, layer, n_pages):
    n = pl.program_id(0)
    q3 = (q_ref[0] * (1.0 / math.sqrt(HEAD_DIM)))[None]
    nh = q_ref.shape[1]

    def cond(st):
        p, carry, _ = st
        return (p >= 0) & (jnp.max(carry) > SB_DEAD)

    def body(st):
        p, carry, acc = st
        phys = pt_ref[n, p]
        ck = pltpu.make_async_copy(ck_hbm.at[layer, phys], kbuf, sem.at[0])
        cv = pltpu.make_async_copy(cv_hbm.at[layer, phys], vbuf, sem.at[1])
        ck.start()
        cv.start()
        ck.wait()
        cv.wait()
        z = jnp.sum(kbuf[...] * q3, axis=-1, keepdims=True)
        ls, u = _log_sigmoid_pair(z)
        y = u
        k = 1
        while k < PAGE_SIZE:
            y = y + jnp.concatenate([y[k:], jnp.zeros((k, nh, 1), F32)], axis=0)
            k *= 2
        a = jnp.exp(ls + (y - u) + carry)
        acc = acc + jnp.sum(a * vbuf[...], axis=0)
        return p - 1, carry + jnp.sum(u, axis=0, keepdims=True), acc

    st = lax.while_loop(cond, body, (jnp.int32(n_pages - 1), jnp.zeros((1, nh, 1), F32),
                                     jnp.zeros((nh, HEAD_DIM), F32)))
    o_ref[0] = st[2]


def sb_sample(q, cache_k, cache_v, layer, page_table):
    n, nh, hd = q.shape
    n_pages = page_table.shape[1]
    grid_spec = pltpu.PrefetchScalarGridSpec(
        num_scalar_prefetch=1,
        grid=(n,),
        in_specs=[pl.BlockSpec((1, nh, hd), lambda i, pt: (i, 0, 0)),
                  pl.BlockSpec(memory_space=pl.ANY), pl.BlockSpec(memory_space=pl.ANY)],
        out_specs=pl.BlockSpec((1, nh, hd), lambda i, pt: (i, 0, 0)),
        scratch_shapes=[pltpu.VMEM((PAGE_SIZE, nh, hd), F32), pltpu.VMEM((PAGE_SIZE, nh, hd), F32),
                        pltpu.SemaphoreType.DMA((2,))],
    )
    return pl.pallas_call(
        functools.partial(_sb_sample_kernel, layer=layer, n_pages=n_pages),
        grid_spec=grid_spec,
        out_shape=jax.ShapeDtypeStruct((n, nh, hd), F32),
        compiler_params=_cparams(("arbitrary",)),
        name="sb_sample",
    )(page_table, q, cache_k, cache_v)


MOE_TM = 512
MOE_TF = 1792


def _router_kernel(x_ref, sc_ref, sh_ref, wr_ref, br_ref, h_ref, r_ref):
    h = x_ref[...] * (1.0 + sc_ref[0]) + sh_ref[0]
    h_ref[...] = h
    logits = jnp.dot(h, wr_ref[...], precision=HIGHEST, preferred_element_type=F32) + br_ref[...]
    lane = lax.broadcasted_iota(I32, logits.shape, 1)
    logits = jnp.where(lane < N_EXPERTS, logits, NEG)
    m1 = jnp.max(logits, axis=1, keepdims=True)
    lane_f = lane.astype(F32)
    i1 = jnp.min(jnp.where(logits == m1, lane_f, float(LANES)), axis=1, keepdims=True)
    rest = jnp.where(lane_f == i1, NEG, logits)
    m2 = jnp.max(rest, axis=1, keepdims=True)
    i2 = jnp.min(jnp.where(rest == m2, lane_f, float(LANES)), axis=1, keepdims=True)
    e2 = jnp.exp(m2 - m1)
    g1 = 1.0 / (1.0 + e2)
    g2 = e2 / (1.0 + e2)
    r_ref[...] = (jnp.where(lane == 0, i1, 0.0) + jnp.where(lane == 1, i2, 0.0)
                  + jnp.where(lane == 2, g1, 0.0) + jnp.where(lane == 3, g2, 0.0))


def router(x, sc, sh, w_router, b_router, seq, tm):
    m, d = x.shape
    tiles_per_seq = seq // tm
    wr = jnp.zeros((d, LANES), F32).at[:, :N_EXPERTS].set(w_router)
    br = jnp.zeros((1, LANES), F32).at[0, :N_EXPERTS].set(b_router)
    mod_spec = pl.BlockSpec((1, sc.shape[1], d), lambda i: (i // tiles_per_seq, 0, 0))
    return pl.pallas_call(
        _router_kernel,
        grid=(m // tm,),
        in_specs=[pl.BlockSpec((tm, d), lambda i: (i, 0)), mod_spec, mod_spec,
                  pl.BlockSpec((d, LANES), lambda i: (0, 0)), pl.BlockSpec((1, LANES), lambda i: (0, 0))],
        out_specs=[pl.BlockSpec((tm, d), lambda i: (i, 0)), pl.BlockSpec((tm, LANES), lambda i: (i, 0))],
        out_shape=[jax.ShapeDtypeStruct((m, d), F32), jax.ShapeDtypeStruct((m, LANES), F32)],
        compiler_params=_cparams(("parallel",)),
        name="router",
    )(x, sc, sh, wr, br)


def _gather_rows(idx_ref, src_hbm, dst, sem, n):
    def issue(r, c):
        pltpu.make_async_copy(src_hbm.at[pl.ds(idx_ref[r], 1)], dst.at[pl.ds(r, 1)], sem).start()
        return c
    lax.fori_loop(0, n, issue, 0)

    def drain(r, c):
        pltpu.make_async_copy(src_hbm.at[pl.ds(0, 1)], dst.at[pl.ds(0, 1)], sem).wait()
        return c
    lax.fori_loop(0, n, drain, 0)


def _moe_ffn_kernel(te_ref, nu_ref, tok_ref, h_hbm, wg_ref, wu_ref, wd_ref, o_ref, xbuf, xb16, acc, sem):
    i = pl.program_id(0)
    j = pl.program_id(1)
    last = pl.num_programs(1) - 1
    live = i < nu_ref[0]

    @pl.when(live)
    def _():
        @pl.when(j == 0)
        def _():
            _gather_rows(tok_ref, h_hbm, xbuf, sem, MOE_TM)
            xb16[...] = xbuf[...].astype(BF16)
            acc[...] = jnp.zeros_like(acc)

        x = xb16[...]
        a = (_silu(_dot(x, wg_ref[0])) * _dot(x, wu_ref[0])).astype(BF16)
        acc[...] += _dot(a, wd_ref[0])

        @pl.when(j == last)
        def _():
            o_ref[...] = acc[...]

    @pl.when(jnp.logical_not(live) & (j == last))
    def _():
        o_ref[...] = jnp.zeros_like(o_ref)


def moe_expert_rows(h, tok_of_slot, tile_expert, n_used, wg, wu, wd):
    d = h.shape[1]
    s_pad = tok_of_slot.shape[0]
    n_tiles = s_pad // MOE_TM
    n_ff = D_FF_EXPERT // MOE_TF
    grid_spec = pltpu.PrefetchScalarGridSpec(
        num_scalar_prefetch=2,
        grid=(n_tiles, n_ff),
        in_specs=[pl.BlockSpec((MOE_TM,), lambda i, j, te, nu: (i,), memory_space=pltpu.SMEM),
                  pl.BlockSpec(memory_space=pl.ANY),
                  pl.BlockSpec((1, d, MOE_TF), lambda i, j, te, nu: (te[i], 0, j)),
                  pl.BlockSpec((1, d, MOE_TF), lambda i, j, te, nu: (te[i], 0, j)),
                  pl.BlockSpec((1, MOE_TF, d), lambda i, j, te, nu: (te[i], j, 0))],
        out_specs=pl.BlockSpec((MOE_TM, d), lambda i, j, te, nu: (i, 0)),
        scratch_shapes=[pltpu.VMEM((MOE_TM, d), F32), pltpu.VMEM((MOE_TM, d), BF16),
                        pltpu.VMEM((MOE_TM, d), F32), pltpu.SemaphoreType.DMA(())],
    )
    return pl.pallas_call(
        _moe_ffn_kernel,
        grid_spec=grid_spec,
        out_shape=jax.ShapeDtypeStruct((s_pad, d), F32),
        compiler_params=_cparams(("arbitrary", "arbitrary")),
        name="moe_ffn",
    )(tile_expert, n_used, tok_of_slot, h, wg, wu, wd)


MOE_TC = 256


def _moe_combine_kernel(s0_ref, s1_ref, ys_hbm, r_ref, x_ref, gt_ref, g_ref, b_ref, o_ref, y0, y1, sem, *, tc):
    _gather_rows(s0_ref, ys_hbm, y0, sem, tc)
    _gather_rows(s1_ref, ys_hbm, y1, sem, tc)
    r = r_ref[...]
    f = r[:, 2:3] * y0[...] + r[:, 3:4] * y1[...]
    y = ALPHA * x_ref[...] + gt_ref[0] * f
    o_ref[...] = _layer_norm_rows(y, g_ref[...], b_ref[...])


def moe_combine_ln(ys, slot0, slot1, r, x, gt, ln_g, ln_b, seq, tc):
    m, d = x.shape
    tiles_per_seq = seq // tc
    return pl.pallas_call(
        functools.partial(_moe_combine_kernel, tc=tc),
        grid=(m // tc,),
        in_specs=[pl.BlockSpec((tc,), lambda i: (i,), memory_space=pltpu.SMEM),
                  pl.BlockSpec((tc,), lambda i: (i,), memory_space=pltpu.SMEM),
                  pl.BlockSpec(memory_space=pl.ANY),
                  pl.BlockSpec((tc, LANES), lambda i: (i, 0)),
                  pl.BlockSpec((tc, d), lambda i: (i, 0)),
                  pl.BlockSpec((1, gt.shape[1], d), lambda i: (i // tiles_per_seq, 0, 0)),
                  pl.BlockSpec((1, d), lambda i: (0, 0)), pl.BlockSpec((1, d), lambda i: (0, 0))],
        out_specs=pl.BlockSpec((tc, d), lambda i: (i, 0)),
        out_shape=jax.ShapeDtypeStruct((m, d), F32),
        scratch_shapes=[pltpu.VMEM((tc, d), F32), pltpu.VMEM((tc, d), F32), pltpu.SemaphoreType.DMA(())],
        compiler_params=_cparams(("arbitrary",)),
        name="moe_combine_ln",
    )(slot0, slot1, ys, r, x, gt, ln_g.reshape(1, d), ln_b.reshape(1, d))


def moe_routing_tables(r):
    t = r.shape[0]
    experts = r[:, :TOP_K].astype(I32).reshape(-1)
    onehot = (experts[:, None] == jnp.arange(N_EXPERTS, dtype=I32)[None, :]).astype(I32)
    rank = jnp.sum((jnp.cumsum(onehot, axis=0) - onehot) * onehot, axis=1)
    counts = jnp.sum(onehot, axis=0)
    padded = ((counts + MOE_TM - 1) // MOE_TM) * MOE_TM
    ends = jnp.cumsum(padded)
    starts = ends - padded
    slot = starts[experts] + rank
    n_tiles = -(-(TOP_K * t) // MOE_TM) + N_EXPERTS
    s_pad = n_tiles * MOE_TM
    tok_of_slot = jnp.zeros((s_pad,), I32).at[slot].set(jnp.arange(TOP_K * t, dtype=I32) // TOP_K)
    tile_start = jnp.arange(n_tiles, dtype=I32) * MOE_TM
    tile_expert = jnp.minimum(jnp.sum((tile_start[:, None] >= ends[None, :]).astype(I32), axis=1), N_EXPERTS - 1)
    n_used = (ends[-1] // MOE_TM).astype(I32).reshape(1)
    slot2 = slot.reshape(t, TOP_K)
    return tok_of_slot, tile_expert.astype(I32), n_used, slot2[:, 0], slot2[:, 1]


def rope_tables(pos):
    half = ROPE_DIM // 2
    inv = ROPE_THETA ** (-jnp.arange(half, dtype=F32) / half)
    ang = pos.astype(F32)[:, None] * inv[None, :]
    cos, sin = jnp.cos(ang), jnp.sin(ang)
    ones = jnp.ones((pos.shape[0], HEAD_DIM - ROPE_DIM), F32)
    zeros = jnp.zeros_like(ones)
    zh = jnp.zeros_like(sin)
    c64 = jnp.concatenate([cos, cos, ones], axis=1)
    sn64 = jnp.concatenate([-sin, zh, zeros], axis=1)
    sp64 = jnp.concatenate([zh, sin, zeros], axis=1)
    return tuple(jnp.concatenate([t, t], axis=1) for t in (c64, sn64, sp64))


def _even_weight_layout(w_in):
    o = (0,) + EVEN_OFFSETS
    kv = o[7]
    sec = lambda start, width: np.arange(start, start + width)
    pad = lambda width: np.full((width,), -1)
    perm = np.concatenate([
        sec(o[0], 512), sec(o[3], 512), sec(o[6], 512),
        sec(o[1], 128), sec(kv + 2 * 128, 128), sec(kv + 4 * 128, 128), sec(o[4], 64), pad(64),
        sec(o[6], 512), sec(o[2], 128), sec(kv, 128), sec(kv + 128, 128),
        sec(kv + 3 * 128, 128), sec(kv + 5 * 128, 128), sec(o[5], 8), sec(o[8], 24), pad(96)])
    assert perm.shape[0] == EV_W
    w_ext = jnp.concatenate([w_in, jnp.zeros((w_in.shape[0], 1), w_in.dtype)], axis=1)
    return w_ext[:, np.where(perm < 0, w_in.shape[1], perm)].astype(BF16)


def _masked_softmax(s, mask):
    s = jnp.where(mask, s, NEG)
    return jax.nn.softmax(s, axis=-1) * mask


def _shared_attend(q, k, v, mask):
    n, tq, h, d = q.shape
    g = k.shape[2]
    qg = q.reshape(n, tq, g, h // g, d)
    s = jnp.einsum('ntgrd,nsgd->ntgrs', qg, k, precision=HIGHEST) / math.sqrt(d)
    p = _masked_softmax(s, mask[None, :, None, None, :])
    o = jnp.einsum('ntgrs,nsgd->ntgrd', p, v, precision=HIGHEST)
    return o.reshape(n, tq, h, d), p


def _gathered_attend(q, k, v, valid):
    n, tq, h, d = q.shape
    g = k.shape[2]
    qg = q.reshape(n, tq, g, h // g, d)
    s = jnp.einsum('ntgrd,ntgsd->ntgrs', qg, k, precision=HIGHEST) / math.sqrt(d)
    p = _masked_softmax(s, valid[:, :, :, None, :])
    o = jnp.einsum('ntgrs,ntgsd->ntgrd', p, v, precision=HIGHEST)
    return o.reshape(n, tq, h, d)


def _gather_pages(pool, layer, page_table):
    g = pool[layer, page_table]
    return g.reshape((g.shape[0], g.shape[1] * g.shape[2]) + g.shape[3:])


def _dsa_sample(qa, qi, wi, ka, va, ki, cache_k, cache_v, cache_kidx, layer, page_table, pos):
    n, tq = qa.shape[:2]
    past = page_table.shape[1] * PAGE_SIZE
    kidx_all = jnp.concatenate([_gather_pages(cache_kidx, layer, page_table), ki], axis=1)
    topk = min(IDX_TOPK, kidx_all.shape[1] // 4)
    s = jax.nn.relu(jnp.einsum('nthd,nsd->nths', qi, kidx_all, precision=HIGHEST))
    score = jnp.einsum('nth,nths->nts', wi, s, precision=HIGHEST)
    k_pos = jnp.arange(kidx_all.shape[1])
    score = jnp.where((k_pos[None, :] <= pos[:, None])[None], score, NEG)
    _, idx = lax.top_k(score, topk)
    valid = idx <= pos[None, :, None]
    bn = jnp.arange(n)[:, None, None]
    pidx = jnp.minimum(idx, past - 1)
    phys = page_table[bn, pidx // PAGE_SIZE]
    off = pidx % PAGE_SIZE
    nidx = jnp.clip(idx - past, 0, tq - 1)
    is_past = (idx < past)[..., None, None]
    kg = jnp.where(is_past, cache_k[layer, phys, off], ka[bn, nidx]).transpose(0, 1, 3, 2, 4)
    vg = jnp.where(is_past, cache_v[layer, phys, off], va[bn, nidx]).transpose(0, 1, 3, 2, 4)
    return _gathered_attend(qa, kg, vg, valid[:, :, None, :])


def _compress_sample(rows, pos_emb, w1, w2):
    n, l, g, d = rows.shape
    n_cmp = (l - CMP_LEN) // CMP_STRIDE + 1
    r = CMP_LEN // CMP_STRIDE
    chunks = rows[:, :(n_cmp + r - 1) * CMP_STRIDE].reshape(n, n_cmp + r - 1, CMP_STRIDE, g, d)
    w1c = w1.reshape(r, CMP_STRIDE, d, CMP_HID)
    h = jnp.matmul(pos_emb.reshape(-1), w1, precision=HIGHEST)
    for j in range(r):
        h = h + jnp.einsum('nclgd,ldh->ncgh', chunks[:, j:j + n_cmp], w1c[j], precision=HIGHEST)
    return jnp.matmul(jax.nn.silu(h), w2, precision=HIGHEST)


def _to_blocks(rows):
    n, l, g, d = rows.shape
    n_slc = -(-l // SLC_LEN)
    rows = jnp.pad(rows, ((0, 0), (0, n_slc * SLC_LEN - l), (0, 0), (0, 0)))
    return rows.reshape(n, n_slc, SLC_LEN, g, d)


def _cmp_to_slc(n_cmp, n_slc):
    i = np.arange(n_cmp)[:, None] * CMP_STRIDE
    j = np.arange(n_slc)[None, :] * SLC_LEN
    return jnp.asarray((i < j + SLC_LEN) & (i + CMP_LEN > j), dtype=F32)


def _nsa_attend_sample(q_rot, q_raw, gates, q_pos, kc, vc, ks_blk, vs_blk, kw, vw, kw_pos):
    n, tq, h, d = q_rot.shape
    g = kc.shape[2]
    n_cmp, n_slc = kc.shape[1], ks_blk.shape[1]
    cmp_end = jnp.arange(n_cmp) * CMP_STRIDE + (CMP_LEN - 1)
    o_cmp, p_cmp = _shared_attend(q_raw, kc, vc, cmp_end[None, :] <= q_pos[:, None])
    p_slc = jnp.einsum('ntgrc,cj->ntgj', p_cmp, _cmp_to_slc(n_cmp, n_slc), precision=HIGHEST)
    blk = jnp.arange(n_slc)[None, :]
    cur = (q_pos // SLC_LEN)[:, None]
    forced = (blk == 0) | (blk == cur) | (blk == cur - 1)
    avail = blk * SLC_LEN <= q_pos[:, None]
    score = jnp.where(avail[None, :, None, :], p_slc + FORCE_BONUS * forced[None, :, None, :], NEG)
    _, sel = lax.top_k(score, min(SLC_TOPN, n_slc))
    bn = jnp.arange(n)[:, None, None, None]
    gi = jnp.arange(g)[None, None, :, None]
    ks = ks_blk[bn, sel, :, gi].reshape(n, tq, g, -1, d)
    vs = vs_blk[bn, sel, :, gi].reshape(n, tq, g, -1, d)
    sel_pos = (sel[..., None] * SLC_LEN + jnp.arange(SLC_LEN)).reshape(n, tq, g, -1)
    o_slc = _gathered_attend(q_rot, ks, vs, sel_pos <= q_pos[None, :, None, None])
    wmask = ((kw_pos[None, :] <= q_pos[:, None]) & (kw_pos[None, :] > q_pos[:, None] - WINDOW)
             & (kw_pos[None, :] >= 0))
    o_win, _ = _shared_attend(q_rot, kw, vw, wmask)
    gt = jax.nn.sigmoid(gates)
    return gt[..., 0:1] * o_cmp + gt[..., 1:2] * o_slc + gt[..., 2:3] * o_win


def _nsa_sample(q_rot, qb, gb, kc, vc, ks, vs, kw, vw, pos, cache_ck, cache_cv, cache_sk, cache_sv,
                win_k, win_v, layer, page_table, cmp_k, cmp_v):
    cat = lambda pool, new: jnp.concatenate([_gather_pages(pool, layer, page_table), new], axis=1)
    kcc = _compress_sample(cat(cache_ck, kc), *cmp_k)
    vcc = _compress_sample(cat(cache_cv, vc), *cmp_v)
    ks_blk, vs_blk = _to_blocks(cat(cache_sk, ks)), _to_blocks(cat(cache_sv, vs))
    w, tq = win_k.shape[1], qb.shape[1]
    kw_all = jnp.concatenate([win_k, kw], axis=1)
    vw_all = jnp.concatenate([win_v, vw], axis=1)
    kw_pos = pos[0] - w + jnp.arange(w + tq)
    o = _nsa_attend_sample(q_rot, qb, gb, pos, kcc, vcc, ks_blk, vs_blk, kw_all, vw_all, kw_pos)
    return o, kw_all[:, -w:], vw_all[:, -w:]


def kernel(x_prompt, x_sample, cache_a_k, cache_a_v, cache_a_kidx, cache_b_cmp_k, cache_b_cmp_v, cache_b_slc_k, cache_b_slc_v, state_b_win_k, state_b_win_v, cache_c_k, cache_c_v, page_table, c_prompt, c_sample, w_ada_mix, b_ada_mix, ln_mix_g, ln_mix_b, w_ada_ffn, b_ada_ffn, ln_ffn_g, ln_ffn_b, w_in_even, w_out_even, cmp_pos_k, cmp_w1_k, cmp_w2_k, cmp_pos_v, cmp_w1_v, cmp_w2_v, w_ffn_gate, w_ffn_up, w_ffn_down, w_in_odd, w_out_odd, w_router, b_router, w_moe_gate, w_moe_up, w_moe_down):
    n_p, t_p, d = x_prompt.shape
    n_s, t_s = x_sample.shape[:2]
    past = page_table.shape[1] * PAGE_SIZE
    pos_p = jnp.arange(t_p)
    pos_s = past + jnp.arange(t_s)
    m_p = n_p * t_p
    tm = 512

    n_c = n_p + n_s
    mp = -(-n_c // SUBLANES) * SUBLANES
    c_all = jnp.zeros((mp, d), F32).at[:n_p].set(c_prompt).at[n_p:n_c].set(c_sample)
    ada_mix = ada_all(c_all, w_ada_mix, b_ada_mix)
    ada_ffn = ada_all(c_all, w_ada_ffn, b_ada_ffn)

    assert t_s == 1, "the decode kernels handle one new token per sequence"
    m_s = n_s * t_s

    def mods(ada, layer):
        sh, sc, gt = jnp.split(ada[layer], 3, axis=-1)
        pr = tuple(a[:n_p].reshape(n_p, 1, d) for a in (sh, sc, gt))
        sa = tuple(jnp.repeat(a[n_p:n_c], t_s, axis=0).reshape(1, m_s, d) for a in (sh, sc, gt))
        return pr, sa

    xp = x_prompt.reshape(m_p, d)
    xs = x_sample.reshape(m_s, d)
    tabs = rope_tables(pos_p)
    tabs_s = rope_tables(jnp.tile(pos_s, n_s))

    (sh_p, sc_p, gt_p), (sh_s, sc_s, gt_s) = mods(ada_mix, 0)
    w_even = _even_weight_layout(w_in_even[0])
    p0 = mod_project(xp, sc_p, sh_p, w_even, tabs, EV_ROPE_W, t_p, tm, 256)
    col = lambda c, w: p0[:, c:c + w]
    kc_p, vc_p = col(C_KC, 128).reshape(n_p, t_p, 128), col(C_VC, 128).reshape(n_p, t_p, 128)
    kcc = compress_rows(kc_p, cmp_pos_k[0], cmp_w1_k[0], cmp_w2_k[0])
    vcc = compress_rows(vc_p, cmp_pos_v[0], cmp_w1_v[0], cmp_w2_v[0])
    oa = dsa_prompt(p0, n_p, t_p)
    ob = nsa_prompt(p0, kcc, vcc, n_p, t_p)
    w_out0 = w_out_even[0].astype(BF16)
    xp = outproj_ln(oa, 0, ob, 0, w_out0, xp, gt_p, ln_mix_g[0], ln_mix_b[0], t_p, tm)

    kv4 = lambda a: a.reshape(1, n_p, t_p, 2, HEAD_DIM)
    wlen = min(WINDOW, t_p)
    even_p = dict(
        a_k=kv4(col(C_KA, 128)), a_v=kv4(col(C_VA, 128)), a_kidx=col(C_KI, 64).reshape(1, n_p, t_p, IDX_DIM),
        cmp_k=kv4(kc_p), cmp_v=kv4(vc_p), slc_k=kv4(col(C_KS, 128)), slc_v=kv4(col(C_VS, 128)),
        win_k=kv4(col(C_KW, 128))[:, :, -wlen:], win_v=kv4(col(C_VW, 128))[:, :, -wlen:])

    p0s = mod_project(xs, sc_s, sh_s, w_even, tabs_s, EV_ROPE_W, m_s, m_s, 256)
    cs = lambda c, w, *shape: p0s[:, c:c + w].reshape((n_s, t_s) + shape)
    kvs = lambda c: cs(c, 128, 2, HEAD_DIM)
    ka, va, ki = kvs(C_KA), kvs(C_VA), cs(C_KI, IDX_DIM, IDX_DIM)
    kc, vc, ks, vs, kw, vw = kvs(C_KC), kvs(C_VC), kvs(C_KS), kvs(C_VS), kvs(C_KW), kvs(C_VW)
    oa_s = _dsa_sample(cs(C_QA, 512, A_HEADS, HEAD_DIM), cs(C_QI, 512, IDX_HEADS, IDX_DIM), cs(C_WG, IDX_HEADS, IDX_HEADS),
                       ka, va, ki, cache_a_k, cache_a_v, cache_a_kidx, 0, page_table, pos_s)
    cmp_k = (cmp_pos_k[0], cmp_w1_k[0], cmp_w2_k[0])
    cmp_v = (cmp_pos_v[0], cmp_w1_v[0], cmp_w2_v[0])
    ob_s, wk, wv = _nsa_sample(cs(C_QBR, 512, B_HEADS, HEAD_DIM), cs(C_QB, 512, B_HEADS, HEAD_DIM),
                               cs(C_WG + WG_GB_LANE, 3 * B_HEADS, B_HEADS, 3), kc, vc, ks, vs, kw, vw, pos_s,
                               cache_b_cmp_k, cache_b_cmp_v, cache_b_slc_k, cache_b_slc_v,
                               state_b_win_k[0], state_b_win_v[0], 0, page_table, cmp_k, cmp_v)
    xs = outproj_ln(oa_s.reshape(m_s, -1), 0, ob_s.reshape(m_s, -1), 0, w_out0, xs, gt_s,
                    ln_mix_g[0], ln_mix_b[0], m_s, m_s)
    even_s = (ka, va, ki, kc, vc, ks, vs, wk, wv)

    (sh_p, sc_p, gt_p), (sh_s, sc_s, gt_s) = mods(ada_ffn, 0)
    ffn_w = (w_ffn_gate[0].astype(BF16), w_ffn_up[0].astype(BF16), w_ffn_down[0].astype(BF16))
    xp = ffn_ln(xp, sc_p, sh_p, gt_p, *ffn_w, ln_ffn_g[0], ln_ffn_b[0], t_p, tm, D_FF // 2)
    xs = ffn_ln(xs, sc_s, sh_s, gt_s, *ffn_w, ln_ffn_g[0], ln_ffn_b[0], m_s, m_s, D_FF // 2)

    (sh_p, sc_p, gt_p), (sh_s, sc_s, gt_s) = mods(ada_mix, 1)
    no_rope = lambda rows: (jnp.ones((rows, LANES), F32), jnp.zeros((rows, LANES), F32), jnp.zeros((rows, LANES), F32))
    w_odd = w_in_odd[0].astype(BF16)
    w_out1 = w_out_odd[0].astype(BF16)
    p1 = mod_project(xp, sc_p, sh_p, w_odd, no_rope(t_p), 0, t_p, tm, 512)
    osb = sb_prompt(p1, n_p, t_p)
    xp = outproj_ln(osb, 0, osb, 1, w_out1, xp, gt_p, ln_mix_g[1], ln_mix_b[1], t_p, tm)
    hd = C_HEADS * HEAD_DIM
    c_k_p = p1[:, hd:2 * hd].reshape(1, n_p, t_p, C_HEADS, HEAD_DIM)
    c_v_p = p1[:, 2 * hd:].reshape(1, n_p, t_p, C_HEADS, HEAD_DIM)

    p1s = mod_project(xs, sc_s, sh_s, w_odd, no_rope(m_s), 0, m_s, m_s, 512)
    shp = (n_s, t_s, C_HEADS, HEAD_DIM)
    k_s, v_s = p1s[:, hd:2 * hd].reshape(shp), p1s[:, 2 * hd:].reshape(shp)
    osb_s = sb_sample(p1s[:, :hd].reshape(m_s, C_HEADS, HEAD_DIM), cache_c_k, cache_c_v, 0, page_table).reshape(m_s, hd)
    xs = outproj_ln(osb_s, 0, osb_s, 1, w_out1, xs, gt_s, ln_mix_g[1], ln_mix_b[1], m_s, m_s)

    (sh_p, sc_p, gt_p), (sh_s, sc_s, gt_s) = mods(ada_ffn, 1)
    h_p, r_p = router(xp, sc_p, sh_p, w_router[0], b_router[0], t_p, tm)
    h_s, r_s = router(xs, sc_s, sh_s, w_router[0], b_router[0], m_s, m_s)
    tok_of_slot, tile_expert, n_used, slot0, slot1 = moe_routing_tables(jnp.concatenate([r_p, r_s], axis=0))
    ys = moe_expert_rows(jnp.concatenate([h_p, h_s], axis=0), tok_of_slot, tile_expert, n_used,
                         w_moe_gate[0].astype(BF16), w_moe_up[0].astype(BF16), w_moe_down[0].astype(BF16))
    xp = moe_combine_ln(ys, slot0[:m_p], slot1[:m_p], r_p, xp, gt_p, ln_ffn_g[1], ln_ffn_b[1], t_p, MOE_TC)
    xs = moe_combine_ln(ys, slot0[m_p:], slot1[m_p:], r_s, xs, gt_s, ln_ffn_g[1], ln_ffn_b[1], m_s, m_s)

    st = lambda a: a[None]
    (a_k_s, a_v_s, a_kidx_s, b_cmp_k_s, b_cmp_v_s, b_slc_k_s, b_slc_v_s, b_win_k_s, b_win_v_s) = [st(a) for a in even_s]
    return (xp.reshape(n_p, t_p, d), xs.reshape(n_s, t_s, d),
            even_p['a_k'], a_k_s, even_p['a_v'], a_v_s, even_p['a_kidx'], a_kidx_s,
            even_p['cmp_k'], b_cmp_k_s, even_p['cmp_v'], b_cmp_v_s,
            even_p['slc_k'], b_slc_k_s, even_p['slc_v'], b_slc_v_s,
            even_p['win_k'], b_win_k_s, even_p['win_v'], b_win_v_s,
            c_k_p, st(k_s), c_v_p, st(v_s))
```

```python
import functools
import math

import numpy as np
import jax
import jax.numpy as jnp
from jax import lax
from jax.experimental import pallas as pl
from jax.experimental.pallas import tpu as pltpu

D_MODEL = 1024
PAGE_SIZE = 128
HEAD_DIM = 64
ROPE_DIM = HEAD_DIM // 4
ROPE_THETA = 500000.0
Q_BLOCK = 128
A_HEADS = 8
A_KV_HEADS = 2
IDX_HEADS = 8
IDX_DIM = 64
IDX_TOPK = 256
B_HEADS = 8
B_KV_HEADS = 2
CMP_LEN = 32
CMP_STRIDE = 16
CMP_HID = 128
SLC_LEN = 64
SLC_TOPN = 16
WINDOW = 512
FORCE_BONUS = 100.0
C_HEADS = 16
D_FF = 2816
N_EXPERTS = 8
TOP_K = 2
D_FF_EXPERT = 3584
DEPTH = 2
ALPHA = (2 * DEPTH) ** 0.25
LN_EPS = 1e-5
NEG = -1e30
EVEN_SPLIT = (A_HEADS * HEAD_DIM, A_KV_HEADS * HEAD_DIM, A_KV_HEADS * HEAD_DIM,
              IDX_HEADS * IDX_DIM, IDX_DIM, IDX_HEADS,
              B_HEADS * HEAD_DIM, 6 * B_KV_HEADS * HEAD_DIM, 3 * B_HEADS)
EVEN_OFFSETS = tuple(int(v) for v in np.cumsum(EVEN_SPLIT)[:-1])

LANES = 128
SUBLANES = 8
VMEM_LIMIT_BYTES = 56 * 1024 * 1024

F32 = jnp.float32
BF16 = jnp.bfloat16
I32 = jnp.int32
INT_MIN = -2 ** 31
HIGHEST = lax.Precision.HIGHEST

C_QA, C_QI, C_QBR, C_KA, C_KS, C_KW, C_KI = 0, 512, 1024, 1536, 1664, 1792, 1920
EV_ROPE_W = 2048
C_QB, C_VA, C_KC, C_VC, C_VS, C_VW, C_WG = 2048, 2560, 2688, 2816, 2944, 3072, 3200
EV_W = 3328
WG_GB_LANE = IDX_HEADS


def _cparams(sem):
    return pltpu.CompilerParams(dimension_semantics=sem, vmem_limit_bytes=VMEM_LIMIT_BYTES)


def _sortable_key(x):
    b = lax.bitcast_convert_type(x + 0.0, I32)
    return jnp.where(b < 0, b ^ jnp.int32(0x7FFFFFFF), b)


_KEY_NEG = int(np.array(NEG, np.float32).view(np.int32)) ^ 0x7FFFFFFF


def _layer_norm_rows(y, g, b):
    mu = jnp.mean(y, axis=-1, keepdims=True)
    d = y - mu
    var = jnp.mean(d * d, axis=-1, keepdims=True)
    return d * lax.rsqrt(var + LN_EPS) * g + b


def _silu(x):
    return x * jax.nn.sigmoid(x)


def _dot_t(a, b):
    return lax.dot_general(a, b, (((1,), (1,)), ((), ())), preferred_element_type=F32)


def _dot(a, b):
    return jnp.dot(a, b, preferred_element_type=F32)


def _topk_bounds(load, n_dyn, n_total, tail_key, k, rows, cw, idx_bits):
    lane = lax.broadcasted_iota(I32, (rows, cw), 1)
    n_tail = ((n_total - n_dyn) * cw).astype(F32) if not isinstance(n_dyn, int) else float((n_total - n_dyn) * cw)
    dyn_w = n_dyn * cw
    kf = float(k)

    def count(pred):
        def body(c, tot):
            hit = pred(load(c), lane + c * cw).astype(F32)
            for t in range(cw // LANES):
                tot = tot + hit[:, t * LANES:(t + 1) * LANES]
            return tot
        tot = lax.fori_loop(0, n_dyn, body, jnp.zeros((rows, LANES), F32))
        return jnp.sum(tot, axis=1, keepdims=True)

    def bit_body(it, u):
        bit = lax.shift_left(jnp.int32(1), jnp.int32(31) - it)
        cu = u | bit
        cand = cu ^ jnp.int32(INT_MIN)
        cnt = count(lambda keys, idx: keys >= cand) + n_tail * (jnp.int32(tail_key) >= cand).astype(F32)
        return jnp.where(cnt >= kf, cu, u)

    u = lax.fori_loop(0, 32, bit_body, jnp.zeros((rows, 1), I32))
    thr = u ^ jnp.int32(INT_MIN)
    tail_gt = (jnp.int32(tail_key) > thr).astype(F32)
    tail_eq = (jnp.int32(tail_key) == thr).astype(F32)
    c_gt = count(lambda keys, idx: keys > thr) + n_tail * tail_gt
    need = kf - c_gt

    def j_body(it, j0):
        bit = lax.shift_left(jnp.int32(1), jnp.int32(idx_bits - 1) - it)
        cj = j0 | bit
        f = count(lambda keys, idx: (keys == thr) & (idx < cj))
        f = f + tail_eq * jnp.clip((cj - dyn_w).astype(F32), 0.0, n_tail)
        return jnp.where(f < need, cj, j0)

    c_eq = count(lambda keys, idx: keys == thr) + n_tail * tail_eq
    j0 = lax.cond(jnp.max(c_eq - need) > 0.0,
                  lambda: lax.fori_loop(0, idx_bits, j_body, jnp.zeros((rows, 1), I32)),
                  lambda: jnp.full((rows, 1), 2 ** idx_bits - 1, I32))
    return thr, j0


def _ada_kernel(c_ref, w_ref, b_ref, o_ref):
    s = _silu(c_ref[...])
    o_ref[0] = jnp.dot(s, w_ref[0], precision=HIGHEST, preferred_element_type=F32) + b_ref[0]


def ada_all(c_all, w, b):
    mp, d = c_all.shape
    nl, _, n3 = w.shape
    tn = 512
    return pl.pallas_call(
        _ada_kernel,
        grid=(nl, n3 // tn),
        in_specs=[pl.BlockSpec((mp, d), lambda l, j: (0, 0)),
                  pl.BlockSpec((1, d, tn), lambda l, j: (l, 0, j)),
                  pl.BlockSpec((1, 1, tn), lambda l, j: (l, 0, j))],
        out_specs=pl.BlockSpec((1, mp, tn), lambda l, j: (l, 0, j)),
        out_shape=jax.ShapeDtypeStruct((nl, mp, n3), F32),
        compiler_params=_cparams(("arbitrary", "arbitrary")),
        name="ada",
    )(c_all, w, b.reshape(nl, 1, n3))


def _proj_kernel(x_ref, sc_ref, sh_ref, w_ref, cos_ref, sn_ref, sp_ref, o_ref, *, n_rope_tiles, tn):
    j = pl.program_id(1)
    h = (x_ref[...] * (1.0 + sc_ref[0]) + sh_ref[0]).astype(BF16)
    acc = _dot(h, w_ref[...])

    @pl.when(j < n_rope_tiles)
    def _():
        c, sn, sp = cos_ref[...], sn_ref[...], sp_ref[...]
        for k in range(tn // LANES):
            seg = acc[:, k * LANES:(k + 1) * LANES]
            o_ref[:, k * LANES:(k + 1) * LANES] = (
                seg * c + pltpu.roll(seg, LANES - ROPE_DIM // 2, 1) * sn + pltpu.roll(seg, ROPE_DIM // 2, 1) * sp)

    @pl.when(j >= n_rope_tiles)
    def _():
        o_ref[...] = acc


def mod_project(x, sc, sh, w, rope_tabs, rope_width, seq, tm, tn):
    m, d = x.shape
    n = w.shape[1]
    cos_t, sn_t, sp_t = rope_tabs
    tiles_per_seq = seq // tm
    mod_spec = pl.BlockSpec((1, sc.shape[1], d), lambda i, j: (i // tiles_per_seq, 0, 0))
    tab_spec = pl.BlockSpec((tm, LANES), lambda i, j: (i % tiles_per_seq, 0))
    return pl.pallas_call(
        functools.partial(_proj_kernel, n_rope_tiles=rope_width // tn, tn=tn),
        grid=(m // tm, n // tn),
        in_specs=[pl.BlockSpec((tm, d), lambda i, j: (i, 0)), mod_spec, mod_spec,
                  pl.BlockSpec((d, tn), lambda i, j: (0, j)), tab_spec, tab_spec, tab_spec],
        out_specs=pl.BlockSpec((tm, tn), lambda i, j: (i, j)),
        out_shape=jax.ShapeDtypeStruct((m, n), F32),
        compiler_params=_cparams(("parallel", "arbitrary")),
        name="mod_project",
    )(x, sc, sh, w, cos_t, sn_t, sp_t)


def _outproj_ln_kernel(a1_ref, a2_ref, w_ref, x_ref, gt_ref, g_ref, b_ref, o_ref):
    a = jnp.concatenate([a1_ref[...], a2_ref[...]], axis=1).astype(BF16)
    y = ALPHA * x_ref[...] + gt_ref[0] * _dot(a, w_ref[...])
    o_ref[...] = _layer_norm_rows(y, g_ref[...], b_ref[...])


def outproj_ln(a1, a1_blk, a2, a2_blk, w, x, gt, ln_g, ln_b, seq, tm):
    m, d = x.shape
    half = d // 2
    tiles_per_seq = seq // tm
    return pl.pallas_call(
        _outproj_ln_kernel,
        grid=(m // tm,),
        in_specs=[pl.BlockSpec((tm, half), lambda i: (i, a1_blk)),
                  pl.BlockSpec((tm, half), lambda i: (i, a2_blk)),
                  pl.BlockSpec((d, d), lambda i: (0, 0)),
                  pl.BlockSpec((tm, d), lambda i: (i, 0)),
                  pl.BlockSpec((1, gt.shape[1], d), lambda i: (i // tiles_per_seq, 0, 0)),
                  pl.BlockSpec((1, d), lambda i: (0, 0)),
                  pl.BlockSpec((1, d), lambda i: (0, 0))],
        out_specs=pl.BlockSpec((tm, d), lambda i: (i, 0)),
        out_shape=jax.ShapeDtypeStruct((m, d), F32),
        compiler_params=_cparams(("parallel",)),
        name="outproj_ln",
    )(a1, a2, w, x, gt, ln_g.reshape(1, d), ln_b.reshape(1, d))


def _ffn_ln_kernel(x_ref, sc_ref, sh_ref, gt_ref, wg_ref, wu_ref, wd_ref, g_ref, b_ref, o_ref, h_scr, acc_scr):
    j = pl.program_id(1)

    @pl.when(j == 0)
    def _():
        h_scr[...] = (x_ref[...] * (1.0 + sc_ref[0]) + sh_ref[0]).astype(BF16)
        acc_scr[...] = jnp.zeros_like(acc_scr)

    h = h_scr[...]
    a = (_silu(_dot(h, wg_ref[...])) * _dot(h, wu_ref[...])).astype(BF16)
    acc_scr[...] += _dot(a, wd_ref[...])

    @pl.when(j == pl.num_programs(1) - 1)
    def _():
        y = ALPHA * x_ref[...] + gt_ref[0] * acc_scr[...]
        o_ref[...] = _layer_norm_rows(y, g_ref[...], b_ref[...])


def ffn_ln(x, sc, sh, gt, wg, wu, wd, ln_g, ln_b, seq, tm, tf):
    m, d = x.shape
    f = wg.shape[1]
    tiles_per_seq = seq // tm
    mod_spec = pl.BlockSpec((1, sc.shape[1], d), lambda i, j: (i // tiles_per_seq, 0, 0))
    vec_spec = pl.BlockSpec((1, d), lambda i, j: (0, 0))
    return pl.pallas_call(
        _ffn_ln_kernel,
        grid=(m // tm, f // tf),
        in_specs=[pl.BlockSpec((tm, d), lambda i, j: (i, 0)), mod_spec, mod_spec, mod_spec,
                  pl.BlockSpec((d, tf), lambda i, j: (0, j)),
                  pl.BlockSpec((d, tf), lambda i, j: (0, j)),
                  pl.BlockSpec((tf, d), lambda i, j: (j, 0)), vec_spec, vec_spec],
        out_specs=pl.BlockSpec((tm, d), lambda i, j: (i, 0)),
        out_shape=jax.ShapeDtypeStruct((m, d), F32),
        scratch_shapes=[pltpu.VMEM((tm, d), BF16), pltpu.VMEM((tm, d), F32)],
        compiler_params=_cparams(("parallel", "arbitrary")),
        name="ffn_ln",
    )(x, sc, sh, gt, wg, wu, wd, ln_g.reshape(1, d), ln_b.reshape(1, d))


def _head_tile(x, h, dst):
    t = x[:, (h // 2) * LANES:(h // 2 + 1) * LANES]
    if h % 2 != dst:
        t = pltpu.roll(t, HEAD_DIM, 1)
    half = lax.broadcasted_iota(I32, t.shape, 1) // HEAD_DIM
    return jnp.where(half == dst, t, 0.0)


def _stack_group(x, g, heads_per_group, scale):
    tiles = [_head_tile(x, g * heads_per_group + r, g) * scale for r in range(heads_per_group)]
    return jnp.concatenate(tiles, axis=0).astype(BF16)


def _unstack_groups(o_groups, heads_per_group, q):
    n_heads = len(o_groups) * heads_per_group
    lane_half = lax.broadcasted_iota(I32, (q, LANES), 1) // HEAD_DIM
    pairs = []
    for hp in range(n_heads // 2):
        halves = []
        for pos in range(2):
            h = 2 * hp + pos
            g, r = divmod(h, heads_per_group)
            t = o_groups[g][r * q:(r + 1) * q]
            if g != pos:
                t = pltpu.roll(t, HEAD_DIM, 1)
            halves.append(t)
        pairs.append(jnp.where(lane_half == 0, halves[0], halves[1]))
    return jnp.concatenate(pairs, axis=1)


def _flash_step(qg, kc, vc, mk, carry):
    m, l, acc = carry
    s = _dot_t(qg, kc)
    s = jnp.where(mk > 0.5, s, NEG)
    m_new = jnp.maximum(m, jnp.max(s, axis=1, keepdims=True))
    alpha = jnp.exp(m - m_new)
    p = jnp.exp(s - m_new) * mk
    l = alpha * l + jnp.sum(p, axis=1, keepdims=True)
    acc = alpha * acc + _dot(p.astype(BF16), vc)
    return m_new, l, acc


def _flash_init(rows):
    return (jnp.full((rows, 1), NEG, F32), jnp.zeros((rows, 1), F32), jnp.zeros((rows, LANES), F32))


def _flash_out(carry):
    _, l, acc = carry
    return acc / jnp.maximum(l, 1e-30)


DSA_CW = 512


def _dsa_kernel(qa_ref, qi_ref, wg_ref, ki_ref, ka_ref, va_ref, o_ref, key_scr, msk_scr, *, seq, topk):
    q = Q_BLOCK
    i = pl.program_id(1)
    q0 = i * q
    nch = seq // DSA_CW
    n_dyn = (q0 + q + DSA_CW - 1) // DSA_CW
    row = q0 + lax.broadcasted_iota(I32, (q, 1), 0)
    lane = lax.broadcasted_iota(I32, (q, DSA_CW), 1)

    qi = qi_ref[...]
    wi = wg_ref[...]
    qih = []
    for h in range(IDX_HEADS):
        t = qi[:, (h // 2) * LANES:(h // 2 + 1) * LANES]
        if h % 2:
            t = pltpu.roll(t, HEAD_DIM, 1)
        qih.append(t.astype(BF16))

    def score_body(c, _):
        off = pl.multiple_of(c * DSA_CW, DSA_CW)
        kc = ki_ref[pl.ds(off, DSA_CW), :].astype(BF16)
        acc = jnp.zeros((q, DSA_CW), F32)
        for h in range(IDX_HEADS):
            acc = acc + wi[:, h:h + 1] * jnp.maximum(_dot_t(qih[h], kc), 0.0)
        acc = jnp.where(lane + off <= row, acc, NEG)
        key_scr[c] = _sortable_key(acc)
        return 0

    lax.fori_loop(0, n_dyn, score_body, 0)

    thr, j0 = _topk_bounds(lambda c: key_scr[c], n_dyn, nch, _KEY_NEG, topk, q, DSA_CW,
                           int(math.log2(seq)))

    def mask_body(c, _):
        keys = key_scr[c]
        idx = lane + c * DSA_CW
        sel = (keys > thr) | ((keys == thr) & (idx <= j0))
        msk_scr[c] = (sel & (idx <= row)).astype(F32)
        return 0

    lax.fori_loop(0, n_dyn, mask_body, 0)

    qa = qa_ref[...]
    hpg = A_HEADS // A_KV_HEADS
    outs = []
    for g in range(A_KV_HEADS):
        qg = _stack_group(qa, g, hpg, 1.0 / math.sqrt(HEAD_DIM))

        def attn_body(c, carry, qg=qg):
            off = pl.multiple_of(c * DSA_CW, DSA_CW)
            kc = ka_ref[pl.ds(off, DSA_CW), :].astype(BF16)
            vc = va_ref[pl.ds(off, DSA_CW), :].astype(BF16)
            mk = msk_scr[c]
            mk = jnp.concatenate([mk] * hpg, axis=0)
            return _flash_step(qg, kc, vc, mk, carry)

        outs.append(_flash_out(lax.fori_loop(0, n_dyn, attn_body, _flash_init(hpg * q))))
    o_ref[...] = _unstack_groups(outs, hpg, q)


def dsa_prompt(p, n_seq, seq):
    nq = seq // Q_BLOCK
    topk = min(IDX_TOPK, seq // 4)
    qspec = lambda cb: pl.BlockSpec((Q_BLOCK, 512), lambda n, i: (n * nq + i, cb))
    kvspec = lambda cb: pl.BlockSpec((seq, LANES), lambda n, i: (n, cb))
    return pl.pallas_call(
        functools.partial(_dsa_kernel, seq=seq, topk=topk),
        grid=(n_seq, nq),
        in_specs=[qspec(C_QA // 512), qspec(C_QI // 512),
                  pl.BlockSpec((Q_BLOCK, LANES), lambda n, i: (n * nq + i, C_WG // LANES)),
                  kvspec(C_KI // LANES), kvspec(C_KA // LANES), kvspec(C_VA // LANES)],
        out_specs=pl.BlockSpec((Q_BLOCK, 512), lambda n, i: (n * nq + i, 0)),
        out_shape=jax.ShapeDtypeStruct((n_seq * seq, 512), F32),
        scratch_shapes=[pltpu.VMEM((seq // DSA_CW, Q_BLOCK, DSA_CW), I32),
                        pltpu.VMEM((seq // DSA_CW, Q_BLOCK, DSA_CW), F32)],
        compiler_params=_cparams(("parallel", "arbitrary")),
        name="dsa_prompt",
    )(p, p, p, p, p, p)


def _compress_kernel(x_ref, pa_ref, pb_ref, w1a_ref, w1b_ref, w2_ref, o_ref):
    x = x_ref[0]
    nchunk = x.shape[0]
    a = _dot((x + pa_ref[...]).astype(BF16), w1a_ref[...])
    b = _dot((x + pb_ref[...]).astype(BF16), w1b_ref[...])
    h = a + pltpu.roll(b, nchunk - 1, 0)
    o_ref[0] = _dot(_silu(h).astype(BF16), w2_ref[...])


def compress_rows(rows, pos_emb, w1, w2):
    n, l, gd = rows.shape
    g = gd // HEAD_DIM
    nchunk = l // CMP_STRIDE
    x = rows.reshape(n, nchunk, CMP_STRIDE * gd)
    r = CMP_LEN // CMP_STRIDE
    w1c = w1.reshape(r, CMP_STRIDE, HEAD_DIM, CMP_HID)
    eye = jnp.eye(g, dtype=F32)
    big = [jnp.einsum('ldh,gk->lgdkh', w1c[j], eye).reshape(CMP_STRIDE * gd, g * CMP_HID).astype(BF16) for j in range(r)]
    w2big = jnp.einsum('hd,gk->ghkd', w2, eye).reshape(g * CMP_HID, gd).astype(BF16)
    pe = pos_emb.reshape(r, CMP_STRIDE, 1, HEAD_DIM)
    pab = [jnp.broadcast_to(pe[j], (CMP_STRIDE, g, HEAD_DIM)).reshape(1, CMP_STRIDE * gd) for j in range(r)]
    k2 = CMP_STRIDE * gd
    return pl.pallas_call(
        _compress_kernel,
        grid=(n,),
        in_specs=[pl.BlockSpec((1, nchunk, k2), lambda i: (i, 0, 0)),
                  pl.BlockSpec((1, k2), lambda i: (0, 0)), pl.BlockSpec((1, k2), lambda i: (0, 0)),
                  pl.BlockSpec((k2, g * CMP_HID), lambda i: (0, 0)),
                  pl.BlockSpec((k2, g * CMP_HID), lambda i: (0, 0)),
                  pl.BlockSpec((g * CMP_HID, gd), lambda i: (0, 0))],
        out_specs=pl.BlockSpec((1, nchunk, gd), lambda i: (i, 0, 0)),
        out_shape=jax.ShapeDtypeStruct((n, nchunk, gd), F32),
        compiler_params=_cparams(("parallel",)),
        name="compress",
    )(x, pab[0], pab[1], big[0], big[1], w2big)


NSA_CW = 512


def _nsa_kernel(qb_ref, qr_ref, wg_ref, kcc_ref, vcc_ref, ks_ref, vs_ref, kw_ref, vw_ref, o_ref, *, seq):
    q = Q_BLOCK
    i = pl.program_id(1)
    q0 = i * q
    hpg = B_HEADS // B_KV_HEADS
    ncmp = seq // CMP_STRIDE
    nslc = seq // SLC_LEN
    n_dyn = (q0 + q + NSA_CW - 1) // NSA_CW
    scale = 1.0 / math.sqrt(HEAD_DIM)
    row = q0 + lax.broadcasted_iota(I32, (q, 1), 0)

    qb = qb_ref[...]
    qr = qr_ref[...]
    kcc = kcc_ref[0].astype(BF16)
    vcc = vcc_ref[0].astype(BF16)

    cidx = lax.broadcasted_iota(I32, (q, ncmp), 1)
    cmask = (cidx * CMP_STRIDE + (CMP_LEN - 1) <= row).astype(F32)
    cmask_g = jnp.concatenate([cmask] * hpg, axis=0)
    assert nslc <= LANES
    ci = lax.broadcasted_iota(I32, (ncmp, LANES), 0) * CMP_STRIDE
    sj = lax.broadcasted_iota(I32, (ncmp, LANES), 1) * SLC_LEN
    cmp2slc = ((ci < sj + SLC_LEN) & (ci + CMP_LEN > sj) & (sj < seq)).astype(BF16)
    blk = lax.broadcasted_iota(I32, (q, LANES), 1)
    cur = row // SLC_LEN
    forced = ((blk == 0) | (blk == cur) | (blk == cur - 1)).astype(F32)
    avail = (blk * SLC_LEN <= row) & (blk < nslc)

    o_cmp, sel_keys = [], []
    for g in range(B_KV_HEADS):
        qg = _stack_group(qb, g, hpg, scale)
        s = jnp.where(cmask_g > 0.5, _dot_t(qg, kcc), NEG)
        e = jnp.exp(s - jnp.max(s, axis=1, keepdims=True)) * cmask_g
        p = e / jnp.maximum(jnp.sum(e, axis=1, keepdims=True), 1e-30)
        o_cmp.append(_dot(p.astype(BF16), vcc))
        psum = p[0:q]
        for r in range(1, hpg):
            psum = psum + p[r * q:(r + 1) * q]
        p_hi = psum.astype(BF16)
        p_lo = (psum - p_hi.astype(F32)).astype(BF16)
        p_slc = _dot(p_hi, cmp2slc) + _dot(p_lo, cmp2slc)
        score = jnp.where(avail, p_slc + FORCE_BONUS * forced, NEG)
        sel_keys.append(jnp.where(blk < nslc, _sortable_key(score), jnp.int32(INT_MIN)))

    keys = jnp.concatenate(sel_keys, axis=0)
    ntop = min(SLC_TOPN, nslc)
    thr, j0 = _topk_bounds(lambda c: keys, 1, 1, INT_MIN, ntop, B_KV_HEADS * q, LANES, int(math.log2(LANES)))
    lane_b = lax.broadcasted_iota(I32, keys.shape, 1)
    sel = ((keys > thr) | ((keys == thr) & (lane_b <= j0))).astype(BF16)

    lane = lax.broadcasted_iota(I32, (q, NSA_CW), 1)
    eb = lax.broadcasted_iota(I32, (LANES, NSA_CW), 0)
    es = lax.broadcasted_iota(I32, (LANES, NSA_CW), 1)

    o_slc, o_win = [], []
    for g in range(B_KV_HEADS):
        qg = _stack_group(qr, g, hpg, scale)
        bm = sel[g * q:(g + 1) * q]

        def slc_body(c, carry, qg=qg, bm=bm):
            off = pl.multiple_of(c * NSA_CW, NSA_CW)
            kc = ks_ref[pl.ds(off, NSA_CW), :].astype(BF16)
            vc = vs_ref[pl.ds(off, NSA_CW), :].astype(BF16)
            expand = ((es + off) // SLC_LEN == eb).astype(BF16)
            mk = _dot(bm, expand) * (lane + off <= row).astype(F32)
            mk = jnp.concatenate([mk] * hpg, axis=0)
            return _flash_step(qg, kc, vc, mk, carry)

        o_slc.append(_flash_out(lax.fori_loop(0, n_dyn, slc_body, _flash_init(hpg * q))))

        lane_w = lax.broadcasted_iota(I32, (q, q), 1)

        def win_body(kb, carry, qg=qg):
            off = pl.multiple_of(kb * q, q)
            kc = kw_ref[pl.ds(off, q), :].astype(BF16)
            vc = vw_ref[pl.ds(off, q), :].astype(BF16)
            col = lane_w + off
            mk = ((col <= row) & (col > row - WINDOW)).astype(F32)
            mk = jnp.concatenate([mk] * hpg, axis=0)
            return _flash_step(qg, kc, vc, mk, carry)

        kb_lo = jnp.maximum(i - WINDOW // q, 0)
        o_win.append(_flash_out(lax.fori_loop(kb_lo, i + 1, win_body, _flash_init(hpg * q))))

    gate = jax.nn.sigmoid(wg_ref[...])
    outs = []
    for g in range(B_KV_HEADS):
        cols = [[gate[:, WG_GB_LANE + (g * hpg + r) * 3 + b: WG_GB_LANE + (g * hpg + r) * 3 + b + 1]
                 for r in range(hpg)] for b in range(3)]
        gc = [jnp.concatenate(cols[b], axis=0) for b in range(3)]
        outs.append(gc[0] * o_cmp[g] + gc[1] * o_slc[g] + gc[2] * o_win[g])
    o_ref[...] = _unstack_groups(outs, hpg, q)


def nsa_prompt(p, kcc, vcc, n_seq, seq):
    nq = seq // Q_BLOCK
    ncmp = seq // CMP_STRIDE
    qspec = lambda cb: pl.BlockSpec((Q_BLOCK, 512), lambda n, i: (n * nq + i, cb))
    kvspec = lambda cb: pl.BlockSpec((seq, LANES), lambda n, i: (n, cb))
    cspec = pl.BlockSpec((1, ncmp, LANES), lambda n, i: (n, 0, 0))
    return pl.pallas_call(
        functools.partial(_nsa_kernel, seq=seq),
        grid=(n_seq, nq),
        in_specs=[qspec(C_QB // 512), qspec(C_QBR // 512),
                  pl.BlockSpec((Q_BLOCK, LANES), lambda n, i: (n * nq + i, C_WG // LANES)),
                  cspec, cspec,
                  kvspec(C_KS // LANES), kvspec(C_VS // LANES), kvspec(C_KW // LANES), kvspec(C_VW // LANES)],
        out_specs=pl.BlockSpec((Q_BLOCK, 512), lambda n, i: (n * nq + i, 0)),
        out_shape=jax.ShapeDtypeStruct((n_seq * seq, 512), F32),
        compiler_params=_cparams(("parallel", "arbitrary")),
        name="nsa_prompt",
    )(p, p, p, kcc, vcc, p, p, p, p)


SB_DEAD = -110.0
SB_PAIRS = 4


def _log_sigmoid_pair(z):
    ls = jnp.minimum(z, 0.0) - jnp.log(1.0 + jnp.exp(-jnp.abs(z)))
    return ls, ls - z


def _split_bf16(x):
    hi = x.astype(BF16)
    return hi, (x - hi.astype(F32)).astype(BF16)


def _sb_kernel(q_ref, k_ref, v_ref, o_ref):
    q = Q_BLOCK
    i = pl.program_id(2)
    q0 = i * q
    nh = 2 * SB_PAIRS
    row = q0 + (lax.broadcasted_iota(I32, (nh * q, 1), 0) & (q - 1))
    lane = lax.broadcasted_iota(I32, (nh * q, q), 1)
    tri = (lax.broadcasted_iota(I32, (q, q), 0) > lax.broadcasted_iota(I32, (q, q), 1)).astype(BF16)
    half = lax.broadcasted_iota(I32, (q, LANES), 1) // HEAD_DIM
    qq = q_ref[...] * (1.0 / math.sqrt(HEAD_DIM))
    q_cat = []
    for hp in range(SB_PAIRS):
        for h in range(2):
            q_hi, q_lo = _split_bf16(jnp.where(half == h, qq[:, hp * LANES:(hp + 1) * LANES], 0.0))
            q_cat.append(jnp.concatenate([q_hi, q_hi, q_lo], axis=1))

    def cond(st):
        return (st[0] >= 0) & (jnp.max(st[1]) > SB_DEAD)

    def body(st):
        j, carry, accs = st[0], st[1], list(st[2:])
        off = pl.multiple_of(j * q, q)
        mask = lane + off < row
        zs, vbs = [], []
        for hp in range(SB_PAIRS):
            k_hi, k_lo = _split_bf16(k_ref[pl.ds(off, q), hp * LANES:(hp + 1) * LANES])
            k_cat = jnp.concatenate([k_hi, k_lo, k_hi], axis=1)
            vbs.append(v_ref[pl.ds(off, q), hp * LANES:(hp + 1) * LANES].astype(BF16))
            for h in range(2):
                zs.append(_dot_t(q_cat[2 * hp + h], k_cat))
        ls, lsn = _log_sigmoid_pair(jnp.concatenate(zs, axis=0))
        u = jnp.where(mask, lsn, 0.0)
        u_hi, u_lo = _split_bf16(u)
        later = _dot(u_hi, tri) + _dot(u_lo, tri) + carry
        a = jnp.where(mask, jnp.exp(ls + later), 0.0).astype(BF16)
        for hp in range(SB_PAIRS):
            for h in range(2):
                r0 = (2 * hp + h) * q
                accs[hp] = accs[hp] + jnp.where(half == h, _dot(a[r0:r0 + q], vbs[hp]), 0.0)
        return (j - 1, carry + jnp.sum(u, axis=1, keepdims=True), *accs)

    zl = jnp.zeros((q, LANES), F32)
    st = lax.while_loop(cond, body, (i, jnp.zeros((nh * q, 1), F32)) + (zl,) * SB_PAIRS)
    o_ref[...] = jnp.concatenate(st[2:], axis=1)


def sb_prompt(p1, n_seq, seq):
    nq = seq // Q_BLOCK
    width = SB_PAIRS * LANES
    ngrp = C_HEADS * HEAD_DIM // width
    return pl.pallas_call(
        _sb_kernel,
        grid=(n_seq, ngrp, nq),
        in_specs=[pl.BlockSpec((Q_BLOCK, width), lambda n, hg, i: (n * nq + i, hg)),
                  pl.BlockSpec((seq, width), lambda n, hg, i: (n, ngrp + hg)),
                  pl.BlockSpec((seq, width), lambda n, hg, i: (n, 2 * ngrp + hg))],
        out_specs=pl.BlockSpec((Q_BLOCK, width), lambda n, hg, i: (n * nq + i, hg)),
        out_shape=jax.ShapeDtypeStruct((n_seq * seq, C_HEADS * HEAD_DIM), F32),
        compiler_params=_cparams(("parallel", "parallel", "arbitrary")),
        name="sb_prompt",
    )(p1, p1, p1)


def _sb_sample_kernel(pt_ref, q_ref, ck_hbm, cv_hbm, o_ref, kbuf, vbuf, sem, *, layer, n_pages):
    n = pl.program_id(0)
    q3 = q_ref[0] * (1.0 / math.sqrt(HEAD_DIM))
    nh = q_ref.shape[1]
    lane = lax.broadcasted_iota(I32, (nh, PAGE_SIZE), 1)

    def cond(st):
        p, carry, _ = st
        return (p >= 0) & (jnp.max(carry) > SB_DEAD)

    def body(st):
        p, carry, acc = st
        phys = pt_ref[n, p]
        ck = pltpu.make_async_copy(ck_hbm.at[layer, phys], kbuf, sem.at[0])
        cv = pltpu.make_async_copy(cv_hbm.at[layer, phys], vbuf, sem.at[1])
        ck.start()
        cv.start()
        ck.wait()
        cv.wait()
        z = jnp.sum(kbuf[...] * q3, axis=1)
        ls, u = _log_sigmoid_pair(z)
        y = u
        k = 1
        while k < PAGE_SIZE:
            y = y + jnp.where(lane + k < PAGE_SIZE, pltpu.roll(y, PAGE_SIZE - k, 1), 0.0)
            k *= 2
        a = jnp.exp(ls + (y - u) + carry)
        acc = acc + jnp.sum(a[:, None, :] * vbuf[...], axis=2)
        return p - 1, carry + jnp.sum(u, axis=1, keepdims=True), acc

    st = lax.while_loop(cond, body, (jnp.int32(n_pages - 1), jnp.zeros((nh, 1), F32),
                                     jnp.zeros((nh, HEAD_DIM), F32)))
    o_ref[0] = st[2]


def sb_sample(q, cache_k, cache_v, layer, page_table):
    n, nh, hd = q.shape
    n_pages = page_table.shape[1]
    to_hds = lambda c: jnp.transpose(c, (0, 1, 3, 4, 2))
    grid_spec = pltpu.PrefetchScalarGridSpec(
        num_scalar_prefetch=1,
        grid=(n,),
        in_specs=[pl.BlockSpec((1, nh, hd, 1), lambda i, pt: (i, 0, 0, 0)),
                  pl.BlockSpec(memory_space=pl.ANY), pl.BlockSpec(memory_space=pl.ANY)],
        out_specs=pl.BlockSpec((1, nh, hd), lambda i, pt: (i, 0, 0)),
        scratch_shapes=[pltpu.VMEM((nh, hd, PAGE_SIZE), F32), pltpu.VMEM((nh, hd, PAGE_SIZE), F32),
                        pltpu.SemaphoreType.DMA((2,))],
    )
    return pl.pallas_call(
        functools.partial(_sb_sample_kernel, layer=layer, n_pages=n_pages),
        grid_spec=grid_spec,
        out_shape=jax.ShapeDtypeStruct((n, nh, hd), F32),
        compiler_params=_cparams(("arbitrary",)),
        name="sb_sample",
    )(page_table, q.reshape(n, nh, hd, 1), to_hds(cache_k), to_hds(cache_v))


MOE_TM = 512
MOE_TF = 1792


def _router_kernel(x_ref, sc_ref, sh_ref, wr_ref, br_ref, h_ref, r_ref):
    h = x_ref[...] * (1.0 + sc_ref[0]) + sh_ref[0]
    h_ref[...] = h
    logits = jnp.dot(h, wr_ref[...], precision=HIGHEST, preferred_element_type=F32) + br_ref[...]
    lane = lax.broadcasted_iota(I32, logits.shape, 1)
    logits = jnp.where(lane < N_EXPERTS, logits, NEG)
    m1 = jnp.max(logits, axis=1, keepdims=True)
    lane_f = lane.astype(F32)
    i1 = jnp.min(jnp.where(logits == m1, lane_f, float(LANES)), axis=1, keepdims=True)
    rest = jnp.where(lane_f == i1, NEG, logits)
    m2 = jnp.max(rest, axis=1, keepdims=True)
    i2 = jnp.min(jnp.where(rest == m2, lane_f, float(LANES)), axis=1, keepdims=True)
    e2 = jnp.exp(m2 - m1)
    g1 = 1.0 / (1.0 + e2)
    g2 = e2 / (1.0 + e2)
    r_ref[...] = (jnp.where(lane == 0, i1, 0.0) + jnp.where(lane == 1, i2, 0.0)
                  + jnp.where(lane == 2, g1, 0.0) + jnp.where(lane == 3, g2, 0.0))


def router(x, sc, sh, w_router, b_router, seq, tm):
    m, d = x.shape
    tiles_per_seq = seq // tm
    wr = jnp.zeros((d, LANES), F32).at[:, :N_EXPERTS].set(w_router)
    br = jnp.zeros((1, LANES), F32).at[0, :N_EXPERTS].set(b_router)
    mod_spec = pl.BlockSpec((1, sc.shape[1], d), lambda i: (i // tiles_per_seq, 0, 0))
    return pl.pallas_call(
        _router_kernel,
        grid=(m // tm,),
        in_specs=[pl.BlockSpec((tm, d), lambda i: (i, 0)), mod_spec, mod_spec,
                  pl.BlockSpec((d, LANES), lambda i: (0, 0)), pl.BlockSpec((1, LANES), lambda i: (0, 0))],
        out_specs=[pl.BlockSpec((tm, d), lambda i: (i, 0)), pl.BlockSpec((tm, LANES), lambda i: (i, 0))],
        out_shape=[jax.ShapeDtypeStruct((m, d), F32), jax.ShapeDtypeStruct((m, LANES), F32)],
        compiler_params=_cparams(("parallel",)),
        name="router",
    )(x, sc, sh, wr, br)


def _gather_rows(idx_ref, src_hbm, dst, sem, n):
    def issue(r, c):
        pltpu.make_async_copy(src_hbm.at[pl.ds(idx_ref[r], 1)], dst.at[pl.ds(r, 1)], sem).start()
        return c
    lax.fori_loop(0, n, issue, 0)

    def drain(r, c):
        pltpu.make_async_copy(src_hbm.at[pl.ds(0, 1)], dst.at[pl.ds(0, 1)], sem).wait()
        return c
    lax.fori_loop(0, n, drain, 0)


def _moe_ffn_kernel(te_ref, nu_ref, tok_ref, h_hbm, wg_ref, wu_ref, wd_ref, o_ref, xbuf, xb16, acc, sem):
    i = pl.program_id(0)
    j = pl.program_id(1)
    last = pl.num_programs(1) - 1
    live = i < nu_ref[0]

    @pl.when(live)
    def _():
        @pl.when(j == 0)
        def _():
            _gather_rows(tok_ref, h_hbm, xbuf, sem, MOE_TM)
            xb16[...] = xbuf[...].astype(BF16)
            acc[...] = jnp.zeros_like(acc)

        x = xb16[...]
        a = (_silu(_dot(x, wg_ref[0])) * _dot(x, wu_ref[0])).astype(BF16)
        acc[...] += _dot(a, wd_ref[0])

        @pl.when(j == last)
        def _():
            o_ref[...] = acc[...]

    @pl.when(jnp.logical_not(live) & (j == last))
    def _():
        o_ref[...] = jnp.zeros_like(o_ref)


def moe_expert_rows(h, tok_of_slot, tile_expert, n_used, wg, wu, wd):
    d = h.shape[1]
    s_pad = tok_of_slot.shape[0]
    n_tiles = s_pad // MOE_TM
    n_ff = D_FF_EXPERT // MOE_TF
    grid_spec = pltpu.PrefetchScalarGridSpec(
        num_scalar_prefetch=2,
        grid=(n_tiles, n_ff),
        in_specs=[pl.BlockSpec((MOE_TM,), lambda i, j, te, nu: (i,), memory_space=pltpu.SMEM),
                  pl.BlockSpec(memory_space=pl.ANY),
                  pl.BlockSpec((1, d, MOE_TF), lambda i, j, te, nu: (te[i], 0, j)),
                  pl.BlockSpec((1, d, MOE_TF), lambda i, j, te, nu: (te[i], 0, j)),
                  pl.BlockSpec((1, MOE_TF, d), lambda i, j, te, nu: (te[i], j, 0))],
        out_specs=pl.BlockSpec((MOE_TM, d), lambda i, j, te, nu: (i, 0)),
        scratch_shapes=[pltpu.VMEM((MOE_TM, d), F32), pltpu.VMEM((MOE_TM, d), BF16),
                        pltpu.VMEM((MOE_TM, d), F32), pltpu.SemaphoreType.DMA(())],
    )
    return pl.pallas_call(
        _moe_ffn_kernel,
        grid_spec=grid_spec,
        out_shape=jax.ShapeDtypeStruct((s_pad, d), F32),
        compiler_params=_cparams(("arbitrary", "arbitrary")),
        name="moe_ffn",
    )(tile_expert, n_used, tok_of_slot, h, wg, wu, wd)


MOE_TC = 256


def _moe_combine_kernel(s0_ref, s1_ref, ys_hbm, r_ref, x_ref, gt_ref, g_ref, b_ref, o_ref, y0, y1, sem, *, tc):
    _gather_rows(s0_ref, ys_hbm, y0, sem, tc)
    _gather_rows(s1_ref, ys_hbm, y1, sem, tc)
    r = r_ref[...]
    f = r[:, 2:3] * y0[...] + r[:, 3:4] * y1[...]
    y = ALPHA * x_ref[...] + gt_ref[0] * f
    o_ref[...] = _layer_norm_rows(y, g_ref[...], b_ref[...])


def moe_combine_ln(ys, slot0, slot1, r, x, gt, ln_g, ln_b, seq, tc):
    m, d = x.shape
    tiles_per_seq = seq // tc
    return pl.pallas_call(
        functools.partial(_moe_combine_kernel, tc=tc),
        grid=(m // tc,),
        in_specs=[pl.BlockSpec((tc,), lambda i: (i,), memory_space=pltpu.SMEM),
                  pl.BlockSpec((tc,), lambda i: (i,), memory_space=pltpu.SMEM),
                  pl.BlockSpec(memory_space=pl.ANY),
                  pl.BlockSpec((tc, LANES), lambda i: (i, 0)),
                  pl.BlockSpec((tc, d), lambda i: (i, 0)),
                  pl.BlockSpec((1, gt.shape[1], d), lambda i: (i // tiles_per_seq, 0, 0)),
                  pl.BlockSpec((1, d), lambda i: (0, 0)), pl.BlockSpec((1, d), lambda i: (0, 0))],
        out_specs=pl.BlockSpec((tc, d), lambda i: (i, 0)),
        out_shape=jax.ShapeDtypeStruct((m, d), F32),
        scratch_shapes=[pltpu.VMEM((tc, d), F32), pltpu.VMEM((tc, d), F32), pltpu.SemaphoreType.DMA(())],
        compiler_params=_cparams(("arbitrary",)),
        name="moe_combine_ln",
    )(slot0, slot1, ys, r, x, gt, ln_g.reshape(1, d), ln_b.reshape(1, d))


def moe_routing_tables(r):
    t = r.shape[0]
    experts = r[:, :TOP_K].astype(I32).reshape(-1)
    onehot = (experts[:, None] == jnp.arange(N_EXPERTS, dtype=I32)[None, :]).astype(I32)
    rank = jnp.sum((jnp.cumsum(onehot, axis=0) - onehot) * onehot, axis=1)
    counts = jnp.sum(onehot, axis=0)
    padded = ((counts + MOE_TM - 1) // MOE_TM) * MOE_TM
    ends = jnp.cumsum(padded)
    starts = ends - padded
    slot = starts[experts] + rank
    n_tiles = -(-(TOP_K * t) // MOE_TM) + N_EXPERTS
    s_pad = n_tiles * MOE_TM
    tok_of_slot = jnp.zeros((s_pad,), I32).at[slot].set(jnp.arange(TOP_K * t, dtype=I32) // TOP_K)
    tile_start = jnp.arange(n_tiles, dtype=I32) * MOE_TM
    tile_expert = jnp.minimum(jnp.sum((tile_start[:, None] >= ends[None, :]).astype(I32), axis=1), N_EXPERTS - 1)
    n_used = (ends[-1] // MOE_TM).astype(I32).reshape(1)
    slot2 = slot.reshape(t, TOP_K)
    return tok_of_slot, tile_expert.astype(I32), n_used, slot2[:, 0], slot2[:, 1]


def rope_tables(pos):
    half = ROPE_DIM // 2
    inv = ROPE_THETA ** (-jnp.arange(half, dtype=F32) / half)
    ang = pos.astype(F32)[:, None] * inv[None, :]
    cos, sin = jnp.cos(ang), jnp.sin(ang)
    ones = jnp.ones((pos.shape[0], HEAD_DIM - ROPE_DIM), F32)
    zeros = jnp.zeros_like(ones)
    zh = jnp.zeros_like(sin)
    c64 = jnp.concatenate([cos, cos, ones], axis=1)
    sn64 = jnp.concatenate([-sin, zh, zeros], axis=1)
    sp64 = jnp.concatenate([zh, sin, zeros], axis=1)
    return tuple(jnp.concatenate([t, t], axis=1) for t in (c64, sn64, sp64))


def _even_weight_layout(w_in):
    o = (0,) + EVEN_OFFSETS
    kv = o[7]
    sec = lambda start, width: np.arange(start, start + width)
    pad = lambda width: np.full((width,), -1)
    perm = np.concatenate([
        sec(o[0], 512), sec(o[3], 512), sec(o[6], 512),
        sec(o[1], 128), sec(kv + 2 * 128, 128), sec(kv + 4 * 128, 128), sec(o[4], 64), pad(64),
        sec(o[6], 512), sec(o[2], 128), sec(kv, 128), sec(kv + 128, 128),
        sec(kv + 3 * 128, 128), sec(kv + 5 * 128, 128), sec(o[5], 8), sec(o[8], 24), pad(96)])
    assert perm.shape[0] == EV_W
    w_ext = jnp.concatenate([w_in, jnp.zeros((w_in.shape[0], 1), w_in.dtype)], axis=1)
    return w_ext[:, np.where(perm < 0, w_in.shape[1], perm)].astype(BF16)


def _masked_softmax(s, mask):
    s = jnp.where(mask, s, NEG)
    return jax.nn.softmax(s, axis=-1) * mask


def _shared_attend(q, k, v, mask):
    n, tq, h, d = q.shape
    g = k.shape[2]
    qg = q.reshape(n, tq, g, h // g, d)
    s = jnp.einsum('ntgrd,nsgd->ntgrs', qg, k, precision=HIGHEST) / math.sqrt(d)
    p = _masked_softmax(s, mask[None, :, None, None, :])
    o = jnp.einsum('ntgrs,nsgd->ntgrd', p, v, precision=HIGHEST)
    return o.reshape(n, tq, h, d), p


def _gathered_attend(q, k, v, valid):
    n, tq, h, d = q.shape
    g = k.shape[2]
    qg = q.reshape(n, tq, g, h // g, d)
    s = jnp.einsum('ntgrd,ntgsd->ntgrs', qg, k, precision=HIGHEST) / math.sqrt(d)
    p = _masked_softmax(s, valid[:, :, :, None, :])
    o = jnp.einsum('ntgrs,ntgsd->ntgrd', p, v, precision=HIGHEST)
    return o.reshape(n, tq, h, d)


def _gather_pages(pool, layer, page_table):
    g = pool[layer, page_table]
    return g.reshape((g.shape[0], g.shape[1] * g.shape[2]) + g.shape[3:])


def _dsa_sample(qa, qi, wi, ka, va, ki, cache_k, cache_v, cache_kidx, layer, page_table, pos):
    n, tq = qa.shape[:2]
    past = page_table.shape[1] * PAGE_SIZE
    kidx_all = jnp.concatenate([_gather_pages(cache_kidx, layer, page_table), ki], axis=1)
    topk = min(IDX_TOPK, kidx_all.shape[1] // 4)
    s = jax.nn.relu(jnp.einsum('nthd,nsd->nths', qi, kidx_all, precision=HIGHEST))
    score = jnp.einsum('nth,nths->nts', wi, s, precision=HIGHEST)
    k_pos = jnp.arange(kidx_all.shape[1])
    score = jnp.where((k_pos[None, :] <= pos[:, None])[None], score, NEG)
    _, idx = lax.top_k(score, topk)
    valid = idx <= pos[None, :, None]
    bn = jnp.arange(n)[:, None, None]
    pidx = jnp.minimum(idx, past - 1)
    phys = page_table[bn, pidx // PAGE_SIZE]
    off = pidx % PAGE_SIZE
    nidx = jnp.clip(idx - past, 0, tq - 1)
    is_past = (idx < past)[..., None, None]
    kg = jnp.where(is_past, cache_k[layer, phys, off], ka[bn, nidx]).transpose(0, 1, 3, 2, 4)
    vg = jnp.where(is_past, cache_v[layer, phys, off], va[bn, nidx]).transpose(0, 1, 3, 2, 4)
    return _gathered_attend(qa, kg, vg, valid[:, :, None, :])


def _to_blocks(rows):
    n, l, g, d = rows.shape
    n_slc = -(-l // SLC_LEN)
    rows = jnp.pad(rows, ((0, 0), (0, n_slc * SLC_LEN - l), (0, 0), (0, 0)))
    return rows.reshape(n, n_slc, SLC_LEN, g, d)


def _cmp_to_slc(n_cmp, n_slc):
    i = np.arange(n_cmp)[:, None] * CMP_STRIDE
    j = np.arange(n_slc)[None, :] * SLC_LEN
    return jnp.asarray((i < j + SLC_LEN) & (i + CMP_LEN > j), dtype=F32)


def _nsa_attend_sample(q_rot, q_raw, gates, q_pos, kc, vc, ks_blk, vs_blk, kw, vw, kw_pos):
    n, tq, h, d = q_rot.shape
    g = kc.shape[2]
    n_cmp, n_slc = kc.shape[1], ks_blk.shape[1]
    cmp_end = jnp.arange(n_cmp) * CMP_STRIDE + (CMP_LEN - 1)
    o_cmp, p_cmp = _shared_attend(q_raw, kc, vc, cmp_end[None, :] <= q_pos[:, None])
    p_slc = jnp.einsum('ntgrc,cj->ntgj', p_cmp, _cmp_to_slc(n_cmp, n_slc), precision=HIGHEST)
    blk = jnp.arange(n_slc)[None, :]
    cur = (q_pos // SLC_LEN)[:, None]
    forced = (blk == 0) | (blk == cur) | (blk == cur - 1)
    avail = blk * SLC_LEN <= q_pos[:, None]
    score = jnp.where(avail[None, :, None, :], p_slc + FORCE_BONUS * forced[None, :, None, :], NEG)
    _, sel = lax.top_k(score, min(SLC_TOPN, n_slc))
    bn = jnp.arange(n)[:, None, None, None]
    gi = jnp.arange(g)[None, None, :, None]
    ks = ks_blk[bn, sel, :, gi].reshape(n, tq, g, -1, d)
    vs = vs_blk[bn, sel, :, gi].reshape(n, tq, g, -1, d)
    sel_pos = (sel[..., None] * SLC_LEN + jnp.arange(SLC_LEN)).reshape(n, tq, g, -1)
    o_slc = _gathered_attend(q_rot, ks, vs, sel_pos <= q_pos[None, :, None, None])
    wmask = ((kw_pos[None, :] <= q_pos[:, None]) & (kw_pos[None, :] > q_pos[:, None] - WINDOW)
             & (kw_pos[None, :] >= 0))
    o_win, _ = _shared_attend(q_rot, kw, vw, wmask)
    gt = jax.nn.sigmoid(gates)
    return gt[..., 0:1] * o_cmp + gt[..., 1:2] * o_slc + gt[..., 2:3] * o_win


def _nsa_sample(q_rot, qb, gb, kc, vc, ks, vs, kw, vw, pos, cache_ck, cache_cv, cache_sk, cache_sv,
                win_k, win_v, layer, page_table, cmp_k, cmp_v):
    cat = lambda pool, new: jnp.concatenate([_gather_pages(pool, layer, page_table), new], axis=1)
    n, tq, g, dh = kc.shape
    past = page_table.shape[1] * PAGE_SIZE
    n_cmp = (past + tq - CMP_LEN) // CMP_STRIDE + 1
    assert (n_cmp + CMP_LEN // CMP_STRIDE - 1) * CMP_STRIDE <= past

    def compress_past(pool, cmp_w):
        rows = _gather_pages(pool, layer, page_table).reshape(n, past, g * dh)
        return compress_rows(rows, *cmp_w)[:, :n_cmp].reshape(n, n_cmp, g, dh)

    kcc = compress_past(cache_ck, cmp_k)
    vcc = compress_past(cache_cv, cmp_v)
    ks_blk, vs_blk = _to_blocks(cat(cache_sk, ks)), _to_blocks(cat(cache_sv, vs))
    w, tq = win_k.shape[1], qb.shape[1]
    kw_all = jnp.concatenate([win_k, kw], axis=1)
    vw_all = jnp.concatenate([win_v, vw], axis=1)
    kw_pos = pos[0] - w + jnp.arange(w + tq)
    o = _nsa_attend_sample(q_rot, qb, gb, pos, kcc, vcc, ks_blk, vs_blk, kw_all, vw_all, kw_pos)
    return o, kw_all[:, -w:], vw_all[:, -w:]


def kernel(x_prompt, x_sample, cache_a_k, cache_a_v, cache_a_kidx, cache_b_cmp_k, cache_b_cmp_v, cache_b_slc_k, cache_b_slc_v, state_b_win_k, state_b_win_v, cache_c_k, cache_c_v, page_table, c_prompt, c_sample, w_ada_mix, b_ada_mix, ln_mix_g, ln_mix_b, w_ada_ffn, b_ada_ffn, ln_ffn_g, ln_ffn_b, w_in_even, w_out_even, cmp_pos_k, cmp_w1_k, cmp_w2_k, cmp_pos_v, cmp_w1_v, cmp_w2_v, w_ffn_gate, w_ffn_up, w_ffn_down, w_in_odd, w_out_odd, w_router, b_router, w_moe_gate, w_moe_up, w_moe_down):
    n_p, t_p, d = x_prompt.shape
    n_s, t_s = x_sample.shape[:2]
    past = page_table.shape[1] * PAGE_SIZE
    pos_p = jnp.arange(t_p)
    pos_s = past + jnp.arange(t_s)
    m_p = n_p * t_p
    tm = 512

    n_c = n_p + n_s
    mp = -(-n_c // SUBLANES) * SUBLANES
    c_all = jnp.zeros((mp, d), F32).at[:n_p].set(c_prompt).at[n_p:n_c].set(c_sample)
    ada_mix = ada_all(c_all, w_ada_mix, b_ada_mix)
    ada_ffn = ada_all(c_all, w_ada_ffn, b_ada_ffn)

    assert t_s == 1, "the decode kernels handle one new token per sequence"
    m_s = n_s * t_s

    def mods(ada, layer):
        sh, sc, gt = jnp.split(ada[layer], 3, axis=-1)
        pr = tuple(a[:n_p].reshape(n_p, 1, d) for a in (sh, sc, gt))
        sa = tuple(jnp.repeat(a[n_p:n_c], t_s, axis=0).reshape(1, m_s, d) for a in (sh, sc, gt))
        return pr, sa

    xp = x_prompt.reshape(m_p, d)
    xs = x_sample.reshape(m_s, d)
    tabs = rope_tables(pos_p)
    tabs_s = rope_tables(jnp.tile(pos_s, n_s))

    (sh_p, sc_p, gt_p), (sh_s, sc_s, gt_s) = mods(ada_mix, 0)
    w_even = _even_weight_layout(w_in_even[0])
    p0 = mod_project(xp, sc_p, sh_p, w_even, tabs, EV_ROPE_W, t_p, tm, 256)
    col = lambda c, w: p0[:, c:c + w]
    kc_p, vc_p = col(C_KC, 128).reshape(n_p, t_p, 128), col(C_VC, 128).reshape(n_p, t_p, 128)
    kcc = compress_rows(kc_p, cmp_pos_k[0], cmp_w1_k[0], cmp_w2_k[0])
    vcc = compress_rows(vc_p, cmp_pos_v[0], cmp_w1_v[0], cmp_w2_v[0])
    oa = dsa_prompt(p0, n_p, t_p)
    ob = nsa_prompt(p0, kcc, vcc, n_p, t_p)
    w_out0 = w_out_even[0].astype(BF16)
    xp = outproj_ln(oa, 0, ob, 0, w_out0, xp, gt_p, ln_mix_g[0], ln_mix_b[0], t_p, tm)

    kv4 = lambda a: a.reshape(1, n_p, t_p, 2, HEAD_DIM)
    wlen = min(WINDOW, t_p)
    even_p = dict(
        a_k=kv4(col(C_KA, 128)), a_v=kv4(col(C_VA, 128)), a_kidx=col(C_KI, 64).reshape(1, n_p, t_p, IDX_DIM),
        cmp_k=kv4(kc_p), cmp_v=kv4(vc_p), slc_k=kv4(col(C_KS, 128)), slc_v=kv4(col(C_VS, 128)),
        win_k=kv4(col(C_KW, 128))[:, :, -wlen:], win_v=kv4(col(C_VW, 128))[:, :, -wlen:])

    p0s = mod_project(xs, sc_s, sh_s, w_even, tabs_s, EV_ROPE_W, m_s, m_s, 256)
    cs = lambda c, w, *shape: p0s[:, c:c + w].reshape((n_s, t_s) + shape)
    kvs = lambda c: cs(c, 128, 2, HEAD_DIM)
    ka, va, ki = kvs(C_KA), kvs(C_VA), cs(C_KI, IDX_DIM, IDX_DIM)
    kc, vc, ks, vs, kw, vw = kvs(C_KC), kvs(C_VC), kvs(C_KS), kvs(C_VS), kvs(C_KW), kvs(C_VW)
    oa_s = _dsa_sample(cs(C_QA, 512, A_HEADS, HEAD_DIM), cs(C_QI, 512, IDX_HEADS, IDX_DIM), cs(C_WG, IDX_HEADS, IDX_HEADS),
                       ka, va, ki, cache_a_k, cache_a_v, cache_a_kidx, 0, page_table, pos_s)
    cmp_k = (cmp_pos_k[0], cmp_w1_k[0], cmp_w2_k[0])
    cmp_v = (cmp_pos_v[0], cmp_w1_v[0], cmp_w2_v[0])
    ob_s, wk, wv = _nsa_sample(cs(C_QBR, 512, B_HEADS, HEAD_DIM), cs(C_QB, 512, B_HEADS, HEAD_DIM),
                               cs(C_WG + WG_GB_LANE, 3 * B_HEADS, B_HEADS, 3), kc, vc, ks, vs, kw, vw, pos_s,
                               cache_b_cmp_k, cache_b_cmp_v, cache_b_slc_k, cache_b_slc_v,
                               state_b_win_k[0], state_b_win_v[0], 0, page_table, cmp_k, cmp_v)
    xs = outproj_ln(oa_s.reshape(m_s, -1), 0, ob_s.reshape(m_s, -1), 0, w_out0, xs, gt_s,
                    ln_mix_g[0], ln_mix_b[0], m_s, m_s)
    even_s = (ka, va, ki, kc, vc, ks, vs, wk, wv)

    (sh_p, sc_p, gt_p), (sh_s, sc_s, gt_s) = mods(ada_ffn, 0)
    ffn_w = (w_ffn_gate[0].astype(BF16), w_ffn_up[0].astype(BF16), w_ffn_down[0].astype(BF16))
    xp = ffn_ln(xp, sc_p, sh_p, gt_p, *ffn_w, ln_ffn_g[0], ln_ffn_b[0], t_p, tm, D_FF // 2)
    xs = ffn_ln(xs, sc_s, sh_s, gt_s, *ffn_w, ln_ffn_g[0], ln_ffn_b[0], m_s, m_s, D_FF // 2)

    (sh_p, sc_p, gt_p), (sh_s, sc_s, gt_s) = mods(ada_mix, 1)
    no_rope = lambda rows: (jnp.ones((rows, LANES), F32), jnp.zeros((rows, LANES), F32), jnp.zeros((rows, LANES), F32))
    w_odd = w_in_odd[0].astype(BF16)
    w_out1 = w_out_odd[0].astype(BF16)
    p1 = mod_project(xp, sc_p, sh_p, w_odd, no_rope(t_p), 0, t_p, tm, 512)
    osb = sb_prompt(p1, n_p, t_p)
    xp = outproj_ln(osb, 0, osb, 1, w_out1, xp, gt_p, ln_mix_g[1], ln_mix_b[1], t_p, tm)
    hd = C_HEADS * HEAD_DIM
    c_k_p = p1[:, hd:2 * hd].reshape(1, n_p, t_p, C_HEADS, HEAD_DIM)
    c_v_p = p1[:, 2 * hd:].reshape(1, n_p, t_p, C_HEADS, HEAD_DIM)

    p1s = mod_project(xs, sc_s, sh_s, w_odd, no_rope(m_s), 0, m_s, m_s, 512)
    shp = (n_s, t_s, C_HEADS, HEAD_DIM)
    k_s, v_s = p1s[:, hd:2 * hd].reshape(shp), p1s[:, 2 * hd:].reshape(shp)
    osb_s = sb_sample(p1s[:, :hd].reshape(m_s, C_HEADS, HEAD_DIM), cache_c_k, cache_c_v, 0, page_table).reshape(m_s, hd)
    xs = outproj_ln(osb_s, 0, osb_s, 1, w_out1, xs, gt_s, ln_mix_g[1], ln_mix_b[1], m_s, m_s)

    (sh_p, sc_p, gt_p), (sh_s, sc_s, gt_s) = mods(ada_ffn, 1)
    h_p, r_p = router(xp, sc_p, sh_p, w_router[0], b_router[0], t_p, tm)
    h_s, r_s = router(xs, sc_s, sh_s, w_router[0], b_router[0], m_s, m_s)
    tok_of_slot, tile_expert, n_used, slot0, slot1 = moe_routing_tables(jnp.concatenate([r_p, r_s], axis=0))
    ys = moe_expert_rows(jnp.concatenate([h_p, h_s], axis=0), tok_of_slot, tile_expert, n_used,
                         w_moe_gate[0].astype(BF16), w_moe_up[0].astype(BF16), w_moe_down[0].astype(BF16))
    xp = moe_combine_ln(ys, slot0[:m_p], slot1[:m_p], r_p, xp, gt_p, ln_ffn_g[1], ln_ffn_b[1], t_p, MOE_TC)
    xs = moe_combine_ln(ys, slot0[m_p:], slot1[m_p:], r_s, xs, gt_s, ln_ffn_g[1], ln_ffn_b[1], m_s, m_s)

    st = lambda a: a[None]
    (a_k_s, a_v_s, a_kidx_s, b_cmp_k_s, b_cmp_v_s, b_slc_k_s, b_slc_v_s, b_win_k_s, b_win_v_s) = [st(a) for a in even_s]
    return (xp.reshape(n_p, t_p, d), xs.reshape(n_s, t_s, d),
            even_p['a_k'], a_k_s, even_p['a_v'], a_v_s, even_p['a_kidx'], a_kidx_s,
            even_p['cmp_k'], b_cmp_k_s, even_p['cmp_v'], b_cmp_v_s,
            even_p['slc_k'], b_slc_k_s, even_p['slc_v'], b_slc_v_s,
            even_p['win_k'], b_win_k_s, even_p['win_v'], b_win_v_s,
            c_k_p, st(k_s), c_v_p, st(v_s))
```

```python
import functools
import math

import numpy as np
import jax
import jax.numpy as jnp
from jax import lax
from jax.experimental import pallas as pl
from jax.experimental.pallas import tpu as pltpu

D_MODEL = 1024
PAGE_SIZE = 128
HEAD_DIM = 64
ROPE_DIM = HEAD_DIM // 4
ROPE_THETA = 500000.0
Q_BLOCK = 128
A_HEADS = 8
A_KV_HEADS = 2
IDX_HEADS = 8
IDX_DIM = 64
IDX_TOPK = 256
B_HEADS = 8
B_KV_HEADS = 2
CMP_LEN = 32
CMP_STRIDE = 16
CMP_HID = 128
SLC_LEN = 64
SLC_TOPN = 16
WINDOW = 512
FORCE_BONUS = 100.0
C_HEADS = 16
D_FF = 2816
N_EXPERTS = 8
TOP_K = 2
D_FF_EXPERT = 3584
DEPTH = 2
ALPHA = (2 * DEPTH) ** 0.25
LN_EPS = 1e-5
NEG = -1e30
EVEN_SPLIT = (A_HEADS * HEAD_DIM, A_KV_HEADS * HEAD_DIM, A_KV_HEADS * HEAD_DIM,
              IDX_HEADS * IDX_DIM, IDX_DIM, IDX_HEADS,
              B_HEADS * HEAD_DIM, 6 * B_KV_HEADS * HEAD_DIM, 3 * B_HEADS)
EVEN_OFFSETS = tuple(int(v) for v in np.cumsum(EVEN_SPLIT)[:-1])

LANES = 128
SUBLANES = 8
VMEM_LIMIT_BYTES = 56 * 1024 * 1024

F32 = jnp.float32
BF16 = jnp.bfloat16
I32 = jnp.int32
INT_MIN = -2 ** 31
HIGHEST = lax.Precision.HIGHEST

C_QA, C_QI, C_QBR, C_KA, C_KS, C_KW, C_KI = 0, 512, 1024, 1536, 1664, 1792, 1920
EV_ROPE_W = 2048
C_QB, C_VA, C_KC, C_VC, C_VS, C_VW, C_WG = 2048, 2560, 2688, 2816, 2944, 3072, 3200
EV_W = 3328
WG_GB_LANE = IDX_HEADS


def _cparams(sem):
    return pltpu.CompilerParams(dimension_semantics=sem, vmem_limit_bytes=VMEM_LIMIT_BYTES)


def _sortable_key(x):
    b = lax.bitcast_convert_type(x + 0.0, I32)
    return jnp.where(b < 0, b ^ jnp.int32(0x7FFFFFFF), b)


_KEY_NEG = int(np.array(NEG, np.float32).view(np.int32)) ^ 0x7FFFFFFF


def _layer_norm_rows(y, g, b):
    mu = jnp.mean(y, axis=-1, keepdims=True)
    d = y - mu
    var = jnp.mean(d * d, axis=-1, keepdims=True)
    return d * lax.rsqrt(var + LN_EPS) * g + b


def _silu(x):
    return x * jax.nn.sigmoid(x)


def _dot_t(a, b):
    return lax.dot_general(a, b, (((1,), (1,)), ((), ())), preferred_element_type=F32)


def _dot(a, b):
    return jnp.dot(a, b, preferred_element_type=F32)


def _topk_bounds(load, n_dyn, n_total, tail_key, k, rows, cw, idx_bits):
    lane = lax.broadcasted_iota(I32, (rows, cw), 1)
    n_tail = ((n_total - n_dyn) * cw).astype(F32) if not isinstance(n_dyn, int) else float((n_total - n_dyn) * cw)
    dyn_w = n_dyn * cw
    kf = float(k)

    def count(pred):
        def body(c, tot):
            hit = pred(load(c), lane + c * cw).astype(F32)
            for t in range(cw // LANES):
                tot = tot + hit[:, t * LANES:(t + 1) * LANES]
            return tot
        tot = lax.fori_loop(0, n_dyn, body, jnp.zeros((rows, LANES), F32))
        return jnp.sum(tot, axis=1, keepdims=True)

    def bit_body(it, u):
        bit = lax.shift_left(jnp.int32(1), jnp.int32(31) - it)
        cu = u | bit
        cand = cu ^ jnp.int32(INT_MIN)
        cnt = count(lambda keys, idx: keys >= cand) + n_tail * (jnp.int32(tail_key) >= cand).astype(F32)
        return jnp.where(cnt >= kf, cu, u)

    u = lax.fori_loop(0, 32, bit_body, jnp.zeros((rows, 1), I32))
    thr = u ^ jnp.int32(INT_MIN)
    tail_gt = (jnp.int32(tail_key) > thr).astype(F32)
    tail_eq = (jnp.int32(tail_key) == thr).astype(F32)
    c_gt = count(lambda keys, idx: keys > thr) + n_tail * tail_gt
    need = kf - c_gt

    def j_body(it, j0):
        bit = lax.shift_left(jnp.int32(1), jnp.int32(idx_bits - 1) - it)
        cj = j0 | bit
        f = count(lambda keys, idx: (keys == thr) & (idx < cj))
        f = f + tail_eq * jnp.clip((cj - dyn_w).astype(F32), 0.0, n_tail)
        return jnp.where(f < need, cj, j0)

    c_eq = count(lambda keys, idx: keys == thr) + n_tail * tail_eq
    j0 = lax.cond(jnp.max(c_eq - need) > 0.0,
                  lambda: lax.fori_loop(0, idx_bits, j_body, jnp.zeros((rows, 1), I32)),
                  lambda: jnp.full((rows, 1), 2 ** idx_bits - 1, I32))
    return thr, j0


def _ada_kernel(c_ref, w_ref, b_ref, o_ref):
    s = _silu(c_ref[...])
    o_ref[0] = jnp.dot(s, w_ref[0], precision=HIGHEST, preferred_element_type=F32) + b_ref[0]


def ada_all(c_all, w, b):
    mp, d = c_all.shape
    nl, _, n3 = w.shape
    tn = 512
    return pl.pallas_call(
        _ada_kernel,
        grid=(nl, n3 // tn),
        in_specs=[pl.BlockSpec((mp, d), lambda l, j: (0, 0)),
                  pl.BlockSpec((1, d, tn), lambda l, j: (l, 0, j)),
                  pl.BlockSpec((1, 1, tn), lambda l, j: (l, 0, j))],
        out_specs=pl.BlockSpec((1, mp, tn), lambda l, j: (l, 0, j)),
        out_shape=jax.ShapeDtypeStruct((nl, mp, n3), F32),
        compiler_params=_cparams(("arbitrary", "arbitrary")),
        name="ada",
    )(c_all, w, b.reshape(nl, 1, n3))


def _proj_kernel(x_ref, sc_ref, sh_ref, w_ref, cos_ref, sn_ref, sp_ref, o_ref, *, n_rope_tiles, tn):
    j = pl.program_id(1)
    h = (x_ref[...] * (1.0 + sc_ref[0]) + sh_ref[0]).astype(BF16)
    acc = _dot(h, w_ref[...])

    @pl.when(j < n_rope_tiles)
    def _():
        c, sn, sp = cos_ref[...], sn_ref[...], sp_ref[...]
        for k in range(tn // LANES):
            seg = acc[:, k * LANES:(k + 1) * LANES]
            o_ref[:, k * LANES:(k + 1) * LANES] = (
                seg * c + pltpu.roll(seg, LANES - ROPE_DIM // 2, 1) * sn + pltpu.roll(seg, ROPE_DIM // 2, 1) * sp)

    @pl.when(j >= n_rope_tiles)
    def _():
        o_ref[...] = acc


def mod_project(x, sc, sh, w, rope_tabs, rope_width, seq, tm, tn):
    m, d = x.shape
    n = w.shape[1]
    cos_t, sn_t, sp_t = rope_tabs
    tiles_per_seq = seq // tm
    mod_spec = pl.BlockSpec((1, sc.shape[1], d), lambda i, j: (i // tiles_per_seq, 0, 0))
    tab_spec = pl.BlockSpec((tm, LANES), lambda i, j: (i % tiles_per_seq, 0))
    return pl.pallas_call(
        functools.partial(_proj_kernel, n_rope_tiles=rope_width // tn, tn=tn),
        grid=(m // tm, n // tn),
        in_specs=[pl.BlockSpec((tm, d), lambda i, j: (i, 0)), mod_spec, mod_spec,
                  pl.BlockSpec((d, tn), lambda i, j: (0, j)), tab_spec, tab_spec, tab_spec],
        out_specs=pl.BlockSpec((tm, tn), lambda i, j: (i, j)),
        out_shape=jax.ShapeDtypeStruct((m, n), F32),
        compiler_params=_cparams(("parallel", "arbitrary")),
        name="mod_project",
    )(x, sc, sh, w, cos_t, sn_t, sp_t)


def _outproj_ln_kernel(a1_ref, a2_ref, w_ref, x_ref, gt_ref, g_ref, b_ref, o_ref):
    a = jnp.concatenate([a1_ref[...], a2_ref[...]], axis=1).astype(BF16)
    y = ALPHA * x_ref[...] + gt_ref[0] * _dot(a, w_ref[...])
    o_ref[...] = _layer_norm_rows(y, g_ref[...], b_ref[...])


def outproj_ln(a1, a1_blk, a2, a2_blk, w, x, gt, ln_g, ln_b, seq, tm):
    m, d = x.shape
    half = d // 2
    tiles_per_seq = seq // tm
    return pl.pallas_call(
        _outproj_ln_kernel,
        grid=(m // tm,),
        in_specs=[pl.BlockSpec((tm, half), lambda i: (i, a1_blk)),
                  pl.BlockSpec((tm, half), lambda i: (i, a2_blk)),
                  pl.BlockSpec((d, d), lambda i: (0, 0)),
                  pl.BlockSpec((tm, d), lambda i: (i, 0)),
                  pl.BlockSpec((1, gt.shape[1], d), lambda i: (i // tiles_per_seq, 0, 0)),
                  pl.BlockSpec((1, d), lambda i: (0, 0)),
                  pl.BlockSpec((1, d), lambda i: (0, 0))],
        out_specs=pl.BlockSpec((tm, d), lambda i: (i, 0)),
        out_shape=jax.ShapeDtypeStruct((m, d), F32),
        compiler_params=_cparams(("parallel",)),
        name="outproj_ln",
    )(a1, a2, w, x, gt, ln_g.reshape(1, d), ln_b.reshape(1, d))


def _ffn_ln_kernel(x_ref, sc_ref, sh_ref, gt_ref, wg_ref, wu_ref, wd_ref, g_ref, b_ref, o_ref, h_scr, acc_scr):
    j = pl.program_id(1)

    @pl.when(j == 0)
    def _():
        h_scr[...] = (x_ref[...] * (1.0 + sc_ref[0]) + sh_ref[0]).astype(BF16)
        acc_scr[...] = jnp.zeros_like(acc_scr)

    h = h_scr[...]
    a = (_silu(_dot(h, wg_ref[...])) * _dot(h, wu_ref[...])).astype(BF16)
    acc_scr[...] += _dot(a, wd_ref[...])

    @pl.when(j == pl.num_programs(1) - 1)
    def _():
        y = ALPHA * x_ref[...] + gt_ref[0] * acc_scr[...]
        o_ref[...] = _layer_norm_rows(y, g_ref[...], b_ref[...])


def ffn_ln(x, sc, sh, gt, wg, wu, wd, ln_g, ln_b, seq, tm, tf):
    m, d = x.shape
    f = wg.shape[1]
    tiles_per_seq = seq // tm
    mod_spec = pl.BlockSpec((1, sc.shape[1], d), lambda i, j: (i // tiles_per_seq, 0, 0))
    vec_spec = pl.BlockSpec((1, d), lambda i, j: (0, 0))
    return pl.pallas_call(
        _ffn_ln_kernel,
        grid=(m // tm, f // tf),
        in_specs=[pl.BlockSpec((tm, d), lambda i, j: (i, 0)), mod_spec, mod_spec, mod_spec,
                  pl.BlockSpec((d, tf), lambda i, j: (0, j)),
                  pl.BlockSpec((d, tf), lambda i, j: (0, j)),
                  pl.BlockSpec((tf, d), lambda i, j: (j, 0)), vec_spec, vec_spec],
        out_specs=pl.BlockSpec((tm, d), lambda i, j: (i, 0)),
        out_shape=jax.ShapeDtypeStruct((m, d), F32),
        scratch_shapes=[pltpu.VMEM((tm, d), BF16), pltpu.VMEM((tm, d), F32)],
        compiler_params=_cparams(("parallel", "arbitrary")),
        name="ffn_ln",
    )(x, sc, sh, gt, wg, wu, wd, ln_g.reshape(1, d), ln_b.reshape(1, d))


def _head_tile(x, h, dst):
    t = x[:, (h // 2) * LANES:(h // 2 + 1) * LANES]
    if h % 2 != dst:
        t = pltpu.roll(t, HEAD_DIM, 1)
    half = lax.broadcasted_iota(I32, t.shape, 1) // HEAD_DIM
    return jnp.where(half == dst, t, 0.0)


def _stack_group(x, g, heads_per_group, scale):
    tiles = [_head_tile(x, g * heads_per_group + r, g) * scale for r in range(heads_per_group)]
    return jnp.concatenate(tiles, axis=0).astype(BF16)


def _unstack_groups(o_groups, heads_per_group, q):
    n_heads = len(o_groups) * heads_per_group
    lane_half = lax.broadcasted_iota(I32, (q, LANES), 1) // HEAD_DIM
    pairs = []
    for hp in range(n_heads // 2):
        halves = []
        for pos in range(2):
            h = 2 * hp + pos
            g, r = divmod(h, heads_per_group)
            t = o_groups[g][r * q:(r + 1) * q]
            if g != pos:
                t = pltpu.roll(t, HEAD_DIM, 1)
            halves.append(t)
        pairs.append(jnp.where(lane_half == 0, halves[0], halves[1]))
    return jnp.concatenate(pairs, axis=1)


def _mask_bias(mk):
    return (mk - 1.0) * (-NEG)


def _flash_step(qg, kc, vc, bias, carry):
    m, l, acc = carry
    s = _dot_t(qg, kc) + bias
    m_new = jnp.maximum(m, jnp.max(s, axis=1, keepdims=True))
    alpha = jnp.exp(m - m_new)
    p = jnp.exp(s - m_new)
    l = alpha * l + jnp.sum(p, axis=1, keepdims=True)
    acc = alpha * acc + _dot(p.astype(BF16), vc)
    return m_new, l, acc


def _flash_init(rows):
    return (jnp.full((rows, 1), NEG, F32), jnp.zeros((rows, 1), F32), jnp.zeros((rows, LANES), F32))


def _flash_out(carry):
    _, l, acc = carry
    return acc / jnp.maximum(l, 1e-30)


DSA_CW = 512


def _dsa_kernel(qa_ref, qi_ref, wg_ref, ki_ref, ka_ref, va_ref, o_ref, key_scr, msk_scr, *, seq, topk):
    q = Q_BLOCK
    i = pl.program_id(1)
    q0 = i * q
    nch = seq // DSA_CW
    n_dyn = (q0 + q + DSA_CW - 1) // DSA_CW
    row = q0 + lax.broadcasted_iota(I32, (q, 1), 0)
    lane = lax.broadcasted_iota(I32, (q, DSA_CW), 1)

    qi = qi_ref[...]
    wi = wg_ref[...]
    qih = []
    for h in range(IDX_HEADS):
        t = qi[:, (h // 2) * LANES:(h // 2 + 1) * LANES]
        if h % 2:
            t = pltpu.roll(t, HEAD_DIM, 1)
        qih.append(t.astype(BF16))

    def score_body(c, _):
        off = pl.multiple_of(c * DSA_CW, DSA_CW)
        kc = ki_ref[pl.ds(off, DSA_CW), :].astype(BF16)
        acc = jnp.zeros((q, DSA_CW), F32)
        for h in range(IDX_HEADS):
            acc = acc + wi[:, h:h + 1] * jnp.maximum(_dot_t(qih[h], kc), 0.0)
        acc = jnp.where(lane + off <= row, acc, NEG)
        key_scr[c] = _sortable_key(acc)
        return 0

    lax.fori_loop(0, n_dyn, score_body, 0)

    thr, j0 = _topk_bounds(lambda c: key_scr[c], n_dyn, nch, _KEY_NEG, topk, q, DSA_CW,
                           int(math.log2(seq)))

    def mask_body(c, _):
        keys = key_scr[c]
        idx = lane + c * DSA_CW
        sel = (keys > thr) | ((keys == thr) & (idx <= j0))
        msk_scr[c] = _mask_bias((sel & (idx <= row)).astype(F32))
        return 0

    lax.fori_loop(0, n_dyn, mask_body, 0)

    qa = qa_ref[...]
    hpg = A_HEADS // A_KV_HEADS
    qgs = [_stack_group(qa, g, hpg, 1.0 / math.sqrt(HEAD_DIM)) for g in range(A_KV_HEADS)]

    def attn_body(c, carries):
        off = pl.multiple_of(c * DSA_CW, DSA_CW)
        kc = ka_ref[pl.ds(off, DSA_CW), :].astype(BF16)
        vc = va_ref[pl.ds(off, DSA_CW), :].astype(BF16)
        bias = jnp.concatenate([msk_scr[c]] * hpg, axis=0)
        return tuple(_flash_step(qgs[g], kc, vc, bias, carries[g]) for g in range(A_KV_HEADS))

    carries = lax.fori_loop(0, n_dyn, attn_body, tuple(_flash_init(hpg * q) for _ in range(A_KV_HEADS)))
    o_ref[...] = _unstack_groups([_flash_out(c) for c in carries], hpg, q)


def dsa_prompt(p, n_seq, seq):
    nq = seq // Q_BLOCK
    topk = min(IDX_TOPK, seq // 4)
    qspec = lambda cb: pl.BlockSpec((Q_BLOCK, 512), lambda n, i: (n * nq + i, cb))
    kvspec = lambda cb: pl.BlockSpec((seq, LANES), lambda n, i: (n, cb))
    return pl.pallas_call(
        functools.partial(_dsa_kernel, seq=seq, topk=topk),
        grid=(n_seq, nq),
        in_specs=[qspec(C_QA // 512), qspec(C_QI // 512),
                  pl.BlockSpec((Q_BLOCK, LANES), lambda n, i: (n * nq + i, C_WG // LANES)),
                  kvspec(C_KI // LANES), kvspec(C_KA // LANES), kvspec(C_VA // LANES)],
        out_specs=pl.BlockSpec((Q_BLOCK, 512), lambda n, i: (n * nq + i, 0)),
        out_shape=jax.ShapeDtypeStruct((n_seq * seq, 512), F32),
        scratch_shapes=[pltpu.VMEM((seq // DSA_CW, Q_BLOCK, DSA_CW), I32),
                        pltpu.VMEM((seq // DSA_CW, Q_BLOCK, DSA_CW), F32)],
        compiler_params=_cparams(("parallel", "arbitrary")),
        name="dsa_prompt",
    )(p, p, p, p, p, p)


def _compress_kernel(x_ref, pa_ref, pb_ref, w1a_ref, w1b_ref, w2_ref, o_ref):
    x = x_ref[0]
    nchunk = x.shape[0]
    a = _dot((x + pa_ref[...]).astype(BF16), w1a_ref[...])
    b = _dot((x + pb_ref[...]).astype(BF16), w1b_ref[...])
    h = a + pltpu.roll(b, nchunk - 1, 0)
    o_ref[0] = _dot(_silu(h).astype(BF16), w2_ref[...])


def _compress_operands(pos_emb, w1, w2, g):
    gd = g * HEAD_DIM
    r = CMP_LEN // CMP_STRIDE
    w1c = w1.reshape(r, CMP_STRIDE, HEAD_DIM, CMP_HID)
    eye = jnp.eye(g, dtype=F32)
    big = [jnp.einsum('ldh,gk->lgdkh', w1c[j], eye).reshape(CMP_STRIDE * gd, g * CMP_HID).astype(BF16) for j in range(r)]
    w2big = jnp.einsum('hd,gk->ghkd', w2, eye).reshape(g * CMP_HID, gd).astype(BF16)
    pe = pos_emb.reshape(r, CMP_STRIDE, 1, HEAD_DIM)
    pab = [jnp.broadcast_to(pe[j], (CMP_STRIDE, g, HEAD_DIM)).reshape(1, CMP_STRIDE * gd) for j in range(r)]
    return pab, big, w2big


def _compress_pages_kernel(x_ref, pa_ref, pb_ref, w1a_ref, w1b_ref, w2_ref, o_ref, xs):
    npg, gd, ps = x_ref.shape[1:]
    nchunk = npg * ps // CMP_STRIDE

    def tr(p, c):
        xs[pl.ds(pl.multiple_of(p * ps, ps), ps), :] = x_ref[0, p].T
        return c

    lax.fori_loop(0, npg, tr, 0)
    a = jnp.zeros((nchunk, w1a_ref.shape[1]), F32)
    b = jnp.zeros((nchunk, w1a_ref.shape[1]), F32)
    for l in range(CMP_STRIDE):
        rows = xs[pl.ds(l, nchunk, stride=CMP_STRIDE), :]
        sl = slice(l * gd, (l + 1) * gd)
        a = a + _dot((rows + pa_ref[:, sl]).astype(BF16), w1a_ref[sl, :])
        b = b + _dot((rows + pb_ref[:, sl]).astype(BF16), w1b_ref[sl, :])
    h = a + pltpu.roll(b, nchunk - 1, 0)
    o_ref[0] = _dot(_silu(h).astype(BF16), w2_ref[...])


def compress_pages(pages, pos_emb, w1, w2):
    n, npg, gd, ps = pages.shape
    g = gd // HEAD_DIM
    nchunk = npg * ps // CMP_STRIDE
    pab, big, w2big = _compress_operands(pos_emb, w1, w2, g)
    k2 = CMP_STRIDE * gd
    return pl.pallas_call(
        _compress_pages_kernel,
        grid=(n,),
        in_specs=[pl.BlockSpec((1, npg, gd, ps), lambda i: (i, 0, 0, 0)),
                  pl.BlockSpec((1, k2), lambda i: (0, 0)), pl.BlockSpec((1, k2), lambda i: (0, 0)),
                  pl.BlockSpec((k2, g * CMP_HID), lambda i: (0, 0)),
                  pl.BlockSpec((k2, g * CMP_HID), lambda i: (0, 0)),
                  pl.BlockSpec((g * CMP_HID, gd), lambda i: (0, 0))],
        out_specs=pl.BlockSpec((1, nchunk, gd), lambda i: (i, 0, 0)),
        out_shape=jax.ShapeDtypeStruct((n, nchunk, gd), F32),
        scratch_shapes=[pltpu.VMEM((npg * ps, gd), F32)],
        compiler_params=_cparams(("parallel",)),
        name="compress_pages",
    )(pages, pab[0], pab[1], big[0], big[1], w2big)


def compress_rows(rows, pos_emb, w1, w2):
    n, l, gd = rows.shape
    g = gd // HEAD_DIM
    nchunk = l // CMP_STRIDE
    x = rows.reshape(n, nchunk, CMP_STRIDE * gd)
    pab, big, w2big = _compress_operands(pos_emb, w1, w2, g)
    k2 = CMP_STRIDE * gd
    return pl.pallas_call(
        _compress_kernel,
        grid=(n,),
        in_specs=[pl.BlockSpec((1, nchunk, k2), lambda i: (i, 0, 0)),
                  pl.BlockSpec((1, k2), lambda i: (0, 0)), pl.BlockSpec((1, k2), lambda i: (0, 0)),
                  pl.BlockSpec((k2, g * CMP_HID), lambda i: (0, 0)),
                  pl.BlockSpec((k2, g * CMP_HID), lambda i: (0, 0)),
                  pl.BlockSpec((g * CMP_HID, gd), lambda i: (0, 0))],
        out_specs=pl.BlockSpec((1, nchunk, gd), lambda i: (i, 0, 0)),
        out_shape=jax.ShapeDtypeStruct((n, nchunk, gd), F32),
        compiler_params=_cparams(("parallel",)),
        name="compress",
    )(x, pab[0], pab[1], big[0], big[1], w2big)


NSA_CW = 512


def _nsa_kernel(qb_ref, qr_ref, wg_ref, kcc_ref, vcc_ref, ks_ref, vs_ref, kw_ref, vw_ref, o_ref, *, seq):
    q = Q_BLOCK
    i = pl.program_id(1)
    q0 = i * q
    hpg = B_HEADS // B_KV_HEADS
    ncmp = seq // CMP_STRIDE
    nslc = seq // SLC_LEN
    n_dyn = (q0 + q + NSA_CW - 1) // NSA_CW
    scale = 1.0 / math.sqrt(HEAD_DIM)
    row = q0 + lax.broadcasted_iota(I32, (q, 1), 0)

    qb = qb_ref[...]
    qr = qr_ref[...]
    kcc = kcc_ref[0].astype(BF16)
    vcc = vcc_ref[0].astype(BF16)

    cidx = lax.broadcasted_iota(I32, (q, ncmp), 1)
    cmask = (cidx * CMP_STRIDE + (CMP_LEN - 1) <= row).astype(F32)
    cmask_g = jnp.concatenate([cmask] * hpg, axis=0)
    assert nslc <= LANES
    ci = lax.broadcasted_iota(I32, (ncmp, LANES), 0) * CMP_STRIDE
    sj = lax.broadcasted_iota(I32, (ncmp, LANES), 1) * SLC_LEN
    cmp2slc = ((ci < sj + SLC_LEN) & (ci + CMP_LEN > sj) & (sj < seq)).astype(BF16)
    blk = lax.broadcasted_iota(I32, (q, LANES), 1)
    cur = row // SLC_LEN
    forced = ((blk == 0) | (blk == cur) | (blk == cur - 1)).astype(F32)
    avail = (blk * SLC_LEN <= row) & (blk < nslc)

    o_cmp, sel_keys = [], []
    for g in range(B_KV_HEADS):
        qg = _stack_group(qb, g, hpg, scale)
        s = jnp.where(cmask_g > 0.5, _dot_t(qg, kcc), NEG)
        e = jnp.exp(s - jnp.max(s, axis=1, keepdims=True)) * cmask_g
        p = e / jnp.maximum(jnp.sum(e, axis=1, keepdims=True), 1e-30)
        o_cmp.append(_dot(p.astype(BF16), vcc))
        psum = p[0:q]
        for r in range(1, hpg):
            psum = psum + p[r * q:(r + 1) * q]
        p_hi = psum.astype(BF16)
        p_lo = (psum - p_hi.astype(F32)).astype(BF16)
        p_slc = _dot(p_hi, cmp2slc) + _dot(p_lo, cmp2slc)
        score = jnp.where(avail, p_slc + FORCE_BONUS * forced, NEG)
        sel_keys.append(jnp.where(blk < nslc, _sortable_key(score), jnp.int32(INT_MIN)))

    keys = jnp.concatenate(sel_keys, axis=0)
    ntop = min(SLC_TOPN, nslc)
    thr, j0 = _topk_bounds(lambda c: keys, 1, 1, INT_MIN, ntop, B_KV_HEADS * q, LANES, int(math.log2(LANES)))
    lane_b = lax.broadcasted_iota(I32, keys.shape, 1)
    sel = ((keys > thr) | ((keys == thr) & (lane_b <= j0))).astype(BF16)

    lane = lax.broadcasted_iota(I32, (q, NSA_CW), 1)
    eb = lax.broadcasted_iota(I32, (LANES, NSA_CW), 0)
    es = lax.broadcasted_iota(I32, (LANES, NSA_CW), 1)

    qgs = [_stack_group(qr, g, hpg, scale) for g in range(B_KV_HEADS)]
    init = tuple(_flash_init(hpg * q) for _ in range(B_KV_HEADS))

    def slc_body(c, carries):
        off = pl.multiple_of(c * NSA_CW, NSA_CW)
        kc = ks_ref[pl.ds(off, NSA_CW), :].astype(BF16)
        vc = vs_ref[pl.ds(off, NSA_CW), :].astype(BF16)
        expand = ((es + off) // SLC_LEN == eb).astype(BF16)
        causal = (lane + off <= row).astype(F32)
        out = []
        for g in range(B_KV_HEADS):
            bias = _mask_bias(_dot(sel[g * q:(g + 1) * q], expand) * causal)
            out.append(_flash_step(qgs[g], kc, vc, jnp.concatenate([bias] * hpg, axis=0), carries[g]))
        return tuple(out)

    o_slc = [_flash_out(c) for c in lax.fori_loop(0, n_dyn, slc_body, init)]

    lane_w = lax.broadcasted_iota(I32, (q, q), 1)

    def win_body(kb, carries):
        off = pl.multiple_of(kb * q, q)
        kc = kw_ref[pl.ds(off, q), :].astype(BF16)
        vc = vw_ref[pl.ds(off, q), :].astype(BF16)
        col = lane_w + off
        bias = _mask_bias(((col <= row) & (col > row - WINDOW)).astype(F32))
        bias = jnp.concatenate([bias] * hpg, axis=0)
        return tuple(_flash_step(qgs[g], kc, vc, bias, carries[g]) for g in range(B_KV_HEADS))

    kb_lo = jnp.maximum(i - WINDOW // q, 0)
    o_win = [_flash_out(c) for c in lax.fori_loop(kb_lo, i + 1, win_body, init)]

    gate = jax.nn.sigmoid(wg_ref[...])
    outs = []
    for g in range(B_KV_HEADS):
        cols = [[gate[:, WG_GB_LANE + (g * hpg + r) * 3 + b: WG_GB_LANE + (g * hpg + r) * 3 + b + 1]
                 for r in range(hpg)] for b in range(3)]
        gc = [jnp.concatenate(cols[b], axis=0) for b in range(3)]
        outs.append(gc[0] * o_cmp[g] + gc[1] * o_slc[g] + gc[2] * o_win[g])
    o_ref[...] = _unstack_groups(outs, hpg, q)


def nsa_prompt(p, kcc, vcc, n_seq, seq):
    nq = seq // Q_BLOCK
    ncmp = seq // CMP_STRIDE
    qspec = lambda cb: pl.BlockSpec((Q_BLOCK, 512), lambda n, i: (n * nq + i, cb))
    kvspec = lambda cb: pl.BlockSpec((seq, LANES), lambda n, i: (n, cb))
    cspec = pl.BlockSpec((1, ncmp, LANES), lambda n, i: (n, 0, 0))
    return pl.pallas_call(
        functools.partial(_nsa_kernel, seq=seq),
        grid=(n_seq, nq),
        in_specs=[qspec(C_QB // 512), qspec(C_QBR // 512),
                  pl.BlockSpec((Q_BLOCK, LANES), lambda n, i: (n * nq + i, C_WG // LANES)),
                  cspec, cspec,
                  kvspec(C_KS // LANES), kvspec(C_VS // LANES), kvspec(C_KW // LANES), kvspec(C_VW // LANES)],
        out_specs=pl.BlockSpec((Q_BLOCK, 512), lambda n, i: (n * nq + i, 0)),
        out_shape=jax.ShapeDtypeStruct((n_seq * seq, 512), F32),
        compiler_params=_cparams(("parallel", "arbitrary")),
        name="nsa_prompt",
    )(p, p, p, kcc, vcc, p, p, p, p)


SB_DEAD = -110.0
SB_PAIRS = 4


def _log_sigmoid_pair(z):
    ls = jnp.minimum(z, 0.0) - jnp.log(1.0 + jnp.exp(-jnp.abs(z)))
    return ls, ls - z


def _split_bf16(x):
    hi = x.astype(BF16)
    return hi, (x - hi.astype(F32)).astype(BF16)


def _sb_kernel(q_ref, k_ref, v_ref, o_ref):
    q = Q_BLOCK
    i = pl.program_id(2)
    q0 = i * q
    nh = 2 * SB_PAIRS
    row = q0 + (lax.broadcasted_iota(I32, (nh * q, 1), 0) & (q - 1))
    lane = lax.broadcasted_iota(I32, (nh * q, q), 1)
    tri = (lax.broadcasted_iota(I32, (q, q), 0) > lax.broadcasted_iota(I32, (q, q), 1)).astype(BF16)
    half = lax.broadcasted_iota(I32, (q, LANES), 1) // HEAD_DIM
    qq = q_ref[...] * (1.0 / math.sqrt(HEAD_DIM))
    q_cat = []
    for hp in range(SB_PAIRS):
        for h in range(2):
            q_hi, q_lo = _split_bf16(jnp.where(half == h, qq[:, hp * LANES:(hp + 1) * LANES], 0.0))
            q_cat.append(jnp.concatenate([q_hi, q_hi, q_lo], axis=1))

    def cond(st):
        return (st[0] >= 0) & (jnp.max(st[1]) > SB_DEAD)

    def body(st):
        j, carry, accs = st[0], st[1], list(st[2:])
        off = pl.multiple_of(j * q, q)
        mask = lane + off < row
        zs, vbs = [], []
        for hp in range(SB_PAIRS):
            k_hi, k_lo = _split_bf16(k_ref[pl.ds(off, q), hp * LANES:(hp + 1) * LANES])
            k_cat = jnp.concatenate([k_hi, k_lo, k_hi], axis=1)
            vbs.append(v_ref[pl.ds(off, q), hp * LANES:(hp + 1) * LANES].astype(BF16))
            for h in range(2):
                zs.append(_dot_t(q_cat[2 * hp + h], k_cat))
        ls, lsn = _log_sigmoid_pair(jnp.concatenate(zs, axis=0))
        u = jnp.where(mask, lsn, 0.0)
        u_hi, u_lo = _split_bf16(u)
        later = _dot(u_hi, tri) + _dot(u_lo, tri) + carry
        a = jnp.where(mask, jnp.exp(ls + later), 0.0).astype(BF16)
        for hp in range(SB_PAIRS):
            for h in range(2):
                r0 = (2 * hp + h) * q
                accs[hp] = accs[hp] + jnp.where(half == h, _dot(a[r0:r0 + q], vbs[hp]), 0.0)
        return (j - 1, carry + jnp.sum(u, axis=1, keepdims=True), *accs)

    zl = jnp.zeros((q, LANES), F32)
    st = lax.while_loop(cond, body, (i, jnp.zeros((nh * q, 1), F32)) + (zl,) * SB_PAIRS)
    o_ref[...] = jnp.concatenate(st[2:], axis=1)


def sb_prompt(p1, n_seq, seq):
    nq = seq // Q_BLOCK
    width = SB_PAIRS * LANES
    ngrp = C_HEADS * HEAD_DIM // width
    return pl.pallas_call(
        _sb_kernel,
        grid=(n_seq, ngrp, nq),
        in_specs=[pl.BlockSpec((Q_BLOCK, width), lambda n, hg, i: (n * nq + i, hg)),
                  pl.BlockSpec((seq, width), lambda n, hg, i: (n, ngrp + hg)),
                  pl.BlockSpec((seq, width), lambda n, hg, i: (n, 2 * ngrp + hg))],
        out_specs=pl.BlockSpec((Q_BLOCK, width), lambda n, hg, i: (n * nq + i, hg)),
        out_shape=jax.ShapeDtypeStruct((n_seq * seq, C_HEADS * HEAD_DIM), F32),
        compiler_params=_cparams(("parallel", "parallel", "arbitrary")),
        name="sb_prompt",
    )(p1, p1, p1)


def _sb_sample_kernel(pt_ref, q_ref, ck_hbm, cv_hbm, o_ref, kbuf, vbuf, sem, *, layer, n_pages):
    n = pl.program_id(0)
    q3 = q_ref[0] * (1.0 / math.sqrt(HEAD_DIM))
    nh = q_ref.shape[1]
    lane = lax.broadcasted_iota(I32, (nh, PAGE_SIZE), 1)

    def cond(st):
        p, carry, _ = st
        return (p >= 0) & (jnp.max(carry) > SB_DEAD)

    def body(st):
        p, carry, acc = st
        phys = pt_ref[n, p]
        ck = pltpu.make_async_copy(ck_hbm.at[layer, phys], kbuf, sem.at[0])
        cv = pltpu.make_async_copy(cv_hbm.at[layer, phys], vbuf, sem.at[1])
        ck.start()
        cv.start()
        ck.wait()
        cv.wait()
        z = jnp.sum(kbuf[...] * q3, axis=1)
        ls, u = _log_sigmoid_pair(z)
        y = u
        k = 1
        while k < PAGE_SIZE:
            y = y + jnp.where(lane + k < PAGE_SIZE, pltpu.roll(y, PAGE_SIZE - k, 1), 0.0)
            k *= 2
        a = jnp.exp(ls + (y - u) + carry)
        acc = acc + jnp.sum(a[:, None, :] * vbuf[...], axis=2)
        return p - 1, carry + jnp.sum(u, axis=1, keepdims=True), acc

    st = lax.while_loop(cond, body, (jnp.int32(n_pages - 1), jnp.zeros((nh, 1), F32),
                                     jnp.zeros((nh, HEAD_DIM), F32)))
    o_ref[0] = st[2]


def sb_sample(q, cache_k, cache_v, layer, page_table):
    n, nh, hd = q.shape
    n_pages = page_table.shape[1]
    to_hds = lambda c: jnp.transpose(c, (0, 1, 3, 4, 2))
    grid_spec = pltpu.PrefetchScalarGridSpec(
        num_scalar_prefetch=1,
        grid=(n,),
        in_specs=[pl.BlockSpec((1, nh, hd, 1), lambda i, pt: (i, 0, 0, 0)),
                  pl.BlockSpec(memory_space=pl.ANY), pl.BlockSpec(memory_space=pl.ANY)],
        out_specs=pl.BlockSpec((1, nh, hd), lambda i, pt: (i, 0, 0)),
        scratch_shapes=[pltpu.VMEM((nh, hd, PAGE_SIZE), F32), pltpu.VMEM((nh, hd, PAGE_SIZE), F32),
                        pltpu.SemaphoreType.DMA((2,))],
    )
    return pl.pallas_call(
        functools.partial(_sb_sample_kernel, layer=layer, n_pages=n_pages),
        grid_spec=grid_spec,
        out_shape=jax.ShapeDtypeStruct((n, nh, hd), F32),
        compiler_params=_cparams(("arbitrary",)),
        name="sb_sample",
    )(page_table, q.reshape(n, nh, hd, 1), to_hds(cache_k), to_hds(cache_v))


def _paged_view(pool, layer, page_table):
    g = jnp.moveaxis(pool[layer, page_table], 2, -1)
    return g.reshape(g.shape[0], g.shape[1], -1, g.shape[-1])


def _dsa_select_kernel(qi_ref, wi_ref, kin_ref, kidx_ref, m_ref, mn_ref, scr, *, topk):
    npg, _, ps = kidx_ref.shape[1:]
    qi = qi_ref[0]
    wi = wi_ref[0]

    def score_body(p, c):
        s = jnp.dot(qi, kidx_ref[0, p], precision=HIGHEST, preferred_element_type=F32)
        scr[pl.ds(p, 1), :] = jnp.sum(wi * jnp.maximum(s, 0.0), axis=0, keepdims=True)
        return c

    lax.fori_loop(0, npg, score_body, 0)
    s_new = jnp.sum(wi * jnp.maximum(jnp.sum(qi * kin_ref[0], axis=1, keepdims=True), 0.0), axis=0, keepdims=True)
    keys = _sortable_key(scr[...])
    key_new = _sortable_key(s_new)
    idx = lax.broadcasted_iota(I32, keys.shape, 0) * ps + lax.broadcasted_iota(I32, keys.shape, 1)
    idx_new = npg * ps
    kf = float(topk)

    def total(hit, hit_new):
        t = jnp.sum(jnp.sum(hit.astype(F32), axis=1, keepdims=True), axis=0, keepdims=True)
        return t + hit_new.astype(F32)

    def bit_body(it, u):
        cu = u | lax.shift_left(jnp.int32(1), jnp.int32(31) - it)
        cand = cu ^ jnp.int32(INT_MIN)
        return jnp.where(total(keys >= cand, key_new >= cand) >= kf, cu, u)

    thr = lax.fori_loop(0, 32, bit_body, jnp.zeros((1, 1), I32)) ^ jnp.int32(INT_MIN)
    need = kf - total(keys > thr, key_new > thr)
    idx_bits = int(math.ceil(math.log2(idx_new + 1)))

    def j_body(it, j0):
        cj = j0 | lax.shift_left(jnp.int32(1), jnp.int32(idx_bits - 1) - it)
        f = total((keys == thr) & (idx < cj), (key_new == thr) & (idx_new < cj))
        return jnp.where(f < need, cj, j0)

    j0 = lax.fori_loop(0, idx_bits, j_body, jnp.zeros((1, 1), I32))
    m_ref[0] = ((keys > thr) | ((keys == thr) & (idx <= j0))).astype(F32)
    sel_new = ((key_new > thr) | ((key_new == thr) & (idx_new <= j0))).astype(F32)
    mn_ref[0] = jnp.broadcast_to(sel_new, (1, LANES))


def dsa_sample_select(qi, wi, ki_new, kidx_pages):
    n, nh, dd = qi.shape
    npg, ps = kidx_pages.shape[1], kidx_pages.shape[3]
    topk = min(IDX_TOPK, (npg * ps + 1) // 4)
    return pl.pallas_call(
        functools.partial(_dsa_select_kernel, topk=topk),
        grid=(n,),
        in_specs=[pl.BlockSpec((1, nh, dd), lambda i: (i, 0, 0)), pl.BlockSpec((1, nh, 1), lambda i: (i, 0, 0)),
                  pl.BlockSpec((1, 1, dd), lambda i: (i, 0, 0)),
                  pl.BlockSpec((1, npg, dd, ps), lambda i: (i, 0, 0, 0))],
        out_specs=[pl.BlockSpec((1, npg, ps), lambda i: (i, 0, 0)), pl.BlockSpec((1, 1, LANES), lambda i: (i, 0, 0))],
        out_shape=[jax.ShapeDtypeStruct((n, npg, ps), F32), jax.ShapeDtypeStruct((n, 1, LANES), F32)],
        scratch_shapes=[pltpu.VMEM((npg, ps), F32)],
        compiler_params=_cparams(("parallel",)),
        name="dsa_sample_select",
    )(qi, wi, ki_new, kidx_pages)


def _paged_attn_kernel(q_ref, kn_ref, vn_ref, mn_ref, m0_ref, m1_ref, k_ref, v_ref, o_ref):
    nh = q_ref.shape[1]
    npg = k_ref.shape[1]
    qg = q_ref[0] * (1.0 / math.sqrt(HEAD_DIM))
    qb = qg.astype(BF16)
    rowgrp = lax.broadcasted_iota(I32, (nh, 1), 0) // (nh // 2)

    def body(p, carry):
        m, l, acc = carry
        mk = jnp.where(rowgrp == 0, m0_ref[0, pl.ds(p, 1), :], m1_ref[0, pl.ds(p, 1), :])
        s = _dot(qb, k_ref[0, p].astype(BF16)) + _mask_bias(mk)
        m_new = jnp.maximum(m, jnp.max(s, axis=1, keepdims=True))
        alpha = jnp.exp(m - m_new)
        pr = jnp.exp(s - m_new)
        return (m_new, alpha * l + jnp.sum(pr, axis=1, keepdims=True),
                alpha * acc + _dot_t(pr.astype(BF16), v_ref[0, p].astype(BF16)))

    m, l, acc = lax.fori_loop(0, npg, body, _flash_init(nh))
    s_n = jnp.sum(qg * kn_ref[0], axis=1, keepdims=True) + _mask_bias(mn_ref[0][:, 0:1])
    m_new = jnp.maximum(m, s_n)
    alpha = jnp.exp(m - m_new)
    p_n = jnp.exp(s_n - m_new)
    res = (alpha * acc + p_n * vn_ref[0]) / (alpha * l + p_n)
    o_ref[0] = jnp.where(rowgrp == 0, res, pltpu.roll(res, HEAD_DIM, 1))


def paged_masked_attention(q_placed, k_new, v_new, mask_new, masks, k_pages, v_pages):
    n, nh, _ = q_placed.shape
    row = lambda a: pl.BlockSpec((1,) + a.shape[1:], lambda i: (i,) + (0,) * (a.ndim - 1))
    ops = (q_placed, k_new, v_new, mask_new, masks[0], masks[1], k_pages, v_pages)
    return pl.pallas_call(
        _paged_attn_kernel,
        grid=(n,),
        in_specs=[row(a) for a in ops],
        out_specs=pl.BlockSpec((1, nh, LANES), lambda i: (i, 0, 0)),
        out_shape=jax.ShapeDtypeStruct((n, nh, LANES), F32),
        compiler_params=_cparams(("parallel",)),
        name="paged_masked_attention",
    )(*ops)


def _place_heads(q, groups):
    n, nh, dd = q.shape
    hpg = nh // groups
    z = jnp.zeros_like(q)
    return jnp.concatenate([jnp.concatenate([q[:, :hpg], z[:, :hpg]], axis=2),
                            jnp.concatenate([z[:, hpg:], q[:, hpg:]], axis=2)], axis=1)


def dsa_sample(p0s, cache_k, cache_v, cache_kidx, layer, page_table):
    n = p0s.shape[0]
    cut = lambda c, w: p0s[:, c:c + w]
    mask, mask_new = dsa_sample_select(cut(C_QI, 512).reshape(n, IDX_HEADS, IDX_DIM), cut(C_WG, IDX_HEADS).reshape(n, IDX_HEADS, 1),
                                       cut(C_KI, IDX_DIM).reshape(n, 1, IDX_DIM), _paged_view(cache_kidx, layer, page_table))
    per_head = lambda a: jnp.broadcast_to(a, (n, A_HEADS, LANES))
    o = paged_masked_attention(_place_heads(cut(C_QA, 512).reshape(n, A_HEADS, HEAD_DIM), A_KV_HEADS),
                               per_head(cut(C_KA, LANES).reshape(n, 1, LANES)), per_head(cut(C_VA, LANES).reshape(n, 1, LANES)),
                               per_head(mask_new), (mask, mask),
                               _paged_view(cache_k, layer, page_table), _paged_view(cache_v, layer, page_table))
    return o[:, :, :HEAD_DIM].reshape(n, A_HEADS * HEAD_DIM)


MOE_TM = 512
MOE_TF = 1792


def _router_kernel(x_ref, sc_ref, sh_ref, wr_ref, br_ref, h_ref, r_ref):
    h = x_ref[...] * (1.0 + sc_ref[0]) + sh_ref[0]
    h_ref[...] = h
    logits = jnp.dot(h, wr_ref[...], precision=HIGHEST, preferred_element_type=F32) + br_ref[...]
    lane = lax.broadcasted_iota(I32, logits.shape, 1)
    logits = jnp.where(lane < N_EXPERTS, logits, NEG)
    m1 = jnp.max(logits, axis=1, keepdims=True)
    lane_f = lane.astype(F32)
    i1 = jnp.min(jnp.where(logits == m1, lane_f, float(LANES)), axis=1, keepdims=True)
    rest = jnp.where(lane_f == i1, NEG, logits)
    m2 = jnp.max(rest, axis=1, keepdims=True)
    i2 = jnp.min(jnp.where(rest == m2, lane_f, float(LANES)), axis=1, keepdims=True)
    e2 = jnp.exp(m2 - m1)
    g1 = 1.0 / (1.0 + e2)
    g2 = e2 / (1.0 + e2)
    r_ref[...] = (jnp.where(lane == 0, i1, 0.0) + jnp.where(lane == 1, i2, 0.0)
                  + jnp.where(lane == 2, g1, 0.0) + jnp.where(lane == 3, g2, 0.0))


def router(x, sc, sh, w_router, b_router, seq, tm):
    m, d = x.shape
    tiles_per_seq = seq // tm
    wr = jnp.zeros((d, LANES), F32).at[:, :N_EXPERTS].set(w_router)
    br = jnp.zeros((1, LANES), F32).at[0, :N_EXPERTS].set(b_router)
    mod_spec = pl.BlockSpec((1, sc.shape[1], d), lambda i: (i // tiles_per_seq, 0, 0))
    return pl.pallas_call(
        _router_kernel,
        grid=(m // tm,),
        in_specs=[pl.BlockSpec((tm, d), lambda i: (i, 0)), mod_spec, mod_spec,
                  pl.BlockSpec((d, LANES), lambda i: (0, 0)), pl.BlockSpec((1, LANES), lambda i: (0, 0))],
        out_specs=[pl.BlockSpec((tm, d), lambda i: (i, 0)), pl.BlockSpec((tm, LANES), lambda i: (i, 0))],
        out_shape=[jax.ShapeDtypeStruct((m, d), F32), jax.ShapeDtypeStruct((m, LANES), F32)],
        compiler_params=_cparams(("parallel",)),
        name="router",
    )(x, sc, sh, wr, br)


def _gather_rows(idx_ref, src_hbm, dst, sem, n):
    def issue(r, c):
        pltpu.make_async_copy(src_hbm.at[pl.ds(idx_ref[r], 1)], dst.at[pl.ds(r, 1)], sem).start()
        return c
    lax.fori_loop(0, n, issue, 0)

    def drain(r, c):
        pltpu.make_async_copy(src_hbm.at[pl.ds(0, 1)], dst.at[pl.ds(0, 1)], sem).wait()
        return c
    lax.fori_loop(0, n, drain, 0)


def _moe_ffn_kernel(te_ref, nu_ref, tok_ref, h_hbm, wg_ref, wu_ref, wd_ref, o_ref, xbuf, xb16, acc, sem):
    i = pl.program_id(0)
    j = pl.program_id(1)
    last = pl.num_programs(1) - 1
    live = i < nu_ref[0]

    @pl.when(live)
    def _():
        @pl.when(j == 0)
        def _():
            _gather_rows(tok_ref, h_hbm, xbuf, sem, MOE_TM)
            xb16[...] = xbuf[...].astype(BF16)
            acc[...] = jnp.zeros_like(acc)

        x = xb16[...]
        a = (_silu(_dot(x, wg_ref[0])) * _dot(x, wu_ref[0])).astype(BF16)
        acc[...] += _dot(a, wd_ref[0])

        @pl.when(j == last)
        def _():
            o_ref[...] = acc[...]

    @pl.when(jnp.logical_not(live) & (j == last))
    def _():
        o_ref[...] = jnp.zeros_like(o_ref)


def moe_expert_rows(h, tok_of_slot, tile_expert, n_used, wg, wu, wd):
    d = h.shape[1]
    s_pad = tok_of_slot.shape[0]
    n_tiles = s_pad // MOE_TM
    n_ff = D_FF_EXPERT // MOE_TF
    grid_spec = pltpu.PrefetchScalarGridSpec(
        num_scalar_prefetch=2,
        grid=(n_tiles, n_ff),
        in_specs=[pl.BlockSpec((MOE_TM,), lambda i, j, te, nu: (i,), memory_space=pltpu.SMEM),
                  pl.BlockSpec(memory_space=pl.ANY),
                  pl.BlockSpec((1, d, MOE_TF), lambda i, j, te, nu: (te[i], 0, j)),
                  pl.BlockSpec((1, d, MOE_TF), lambda i, j, te, nu: (te[i], 0, j)),
                  pl.BlockSpec((1, MOE_TF, d), lambda i, j, te, nu: (te[i], j, 0))],
        out_specs=pl.BlockSpec((MOE_TM, d), lambda i, j, te, nu: (i, 0)),
        scratch_shapes=[pltpu.VMEM((MOE_TM, d), F32), pltpu.VMEM((MOE_TM, d), BF16),
                        pltpu.VMEM((MOE_TM, d), F32), pltpu.SemaphoreType.DMA(())],
    )
    return pl.pallas_call(
        _moe_ffn_kernel,
        grid_spec=grid_spec,
        out_shape=jax.ShapeDtypeStruct((s_pad, d), F32),
        compiler_params=_cparams(("arbitrary", "arbitrary")),
        name="moe_ffn",
    )(tile_expert, n_used, tok_of_slot, h, wg, wu, wd)


MOE_TC = 256


def _moe_combine_kernel(s0_ref, s1_ref, ys_hbm, r_ref, x_ref, gt_ref, g_ref, b_ref, o_ref, y0, y1, sem, *, tc):
    _gather_rows(s0_ref, ys_hbm, y0, sem, tc)
    _gather_rows(s1_ref, ys_hbm, y1, sem, tc)
    r = r_ref[...]
    f = r[:, 2:3] * y0[...] + r[:, 3:4] * y1[...]
    y = ALPHA * x_ref[...] + gt_ref[0] * f
    o_ref[...] = _layer_norm_rows(y, g_ref[...], b_ref[...])


def moe_combine_ln(ys, slot0, slot1, r, x, gt, ln_g, ln_b, seq, tc):
    m, d = x.shape
    tiles_per_seq = seq // tc
    return pl.pallas_call(
        functools.partial(_moe_combine_kernel, tc=tc),
        grid=(m // tc,),
        in_specs=[pl.BlockSpec((tc,), lambda i: (i,), memory_space=pltpu.SMEM),
                  pl.BlockSpec((tc,), lambda i: (i,), memory_space=pltpu.SMEM),
                  pl.BlockSpec(memory_space=pl.ANY),
                  pl.BlockSpec((tc, LANES), lambda i: (i, 0)),
                  pl.BlockSpec((tc, d), lambda i: (i, 0)),
                  pl.BlockSpec((1, gt.shape[1], d), lambda i: (i // tiles_per_seq, 0, 0)),
                  pl.BlockSpec((1, d), lambda i: (0, 0)), pl.BlockSpec((1, d), lambda i: (0, 0))],
        out_specs=pl.BlockSpec((tc, d), lambda i: (i, 0)),
        out_shape=jax.ShapeDtypeStruct((m, d), F32),
        scratch_shapes=[pltpu.VMEM((tc, d), F32), pltpu.VMEM((tc, d), F32), pltpu.SemaphoreType.DMA(())],
        compiler_params=_cparams(("arbitrary",)),
        name="moe_combine_ln",
    )(slot0, slot1, ys, r, x, gt, ln_g.reshape(1, d), ln_b.reshape(1, d))


def moe_routing_tables(r):
    t = r.shape[0]
    experts = r[:, :TOP_K].astype(I32).reshape(-1)
    onehot = (experts[:, None] == jnp.arange(N_EXPERTS, dtype=I32)[None, :]).astype(I32)
    rank = jnp.sum((jnp.cumsum(onehot, axis=0) - onehot) * onehot, axis=1)
    counts = jnp.sum(onehot, axis=0)
    padded = ((counts + MOE_TM - 1) // MOE_TM) * MOE_TM
    ends = jnp.cumsum(padded)
    starts = ends - padded
    slot = starts[experts] + rank
    n_tiles = -(-(TOP_K * t) // MOE_TM) + N_EXPERTS
    s_pad = n_tiles * MOE_TM
    tok_of_slot = jnp.zeros((s_pad,), I32).at[slot].set(jnp.arange(TOP_K * t, dtype=I32) // TOP_K)
    tile_start = jnp.arange(n_tiles, dtype=I32) * MOE_TM
    tile_expert = jnp.minimum(jnp.sum((tile_start[:, None] >= ends[None, :]).astype(I32), axis=1), N_EXPERTS - 1)
    n_used = (ends[-1] // MOE_TM).astype(I32).reshape(1)
    slot2 = slot.reshape(t, TOP_K)
    return tok_of_slot, tile_expert.astype(I32), n_used, slot2[:, 0], slot2[:, 1]


def rope_tables(pos):
    half = ROPE_DIM // 2
    inv = ROPE_THETA ** (-jnp.arange(half, dtype=F32) / half)
    ang = pos.astype(F32)[:, None] * inv[None, :]
    cos, sin = jnp.cos(ang), jnp.sin(ang)
    ones = jnp.ones((pos.shape[0], HEAD_DIM - ROPE_DIM), F32)
    zeros = jnp.zeros_like(ones)
    zh = jnp.zeros_like(sin)
    c64 = jnp.concatenate([cos, cos, ones], axis=1)
    sn64 = jnp.concatenate([-sin, zh, zeros], axis=1)
    sp64 = jnp.concatenate([zh, sin, zeros], axis=1)
    return tuple(jnp.concatenate([t, t], axis=1) for t in (c64, sn64, sp64))


def _even_weight_layout(w_in):
    o = (0,) + EVEN_OFFSETS
    kv = o[7]
    sec = lambda start, width: np.arange(start, start + width)
    pad = lambda width: np.full((width,), -1)
    perm = np.concatenate([
        sec(o[0], 512), sec(o[3], 512), sec(o[6], 512),
        sec(o[1], 128), sec(kv + 2 * 128, 128), sec(kv + 4 * 128, 128), sec(o[4], 64), pad(64),
        sec(o[6], 512), sec(o[2], 128), sec(kv, 128), sec(kv + 128, 128),
        sec(kv + 3 * 128, 128), sec(kv + 5 * 128, 128), sec(o[5], 8), sec(o[8], 24), pad(96)])
    assert perm.shape[0] == EV_W
    w_ext = jnp.concatenate([w_in, jnp.zeros((w_in.shape[0], 1), w_in.dtype)], axis=1)
    return w_ext[:, np.where(perm < 0, w_in.shape[1], perm)].astype(BF16)


def _masked_softmax(s, mask):
    s = jnp.where(mask, s, NEG)
    return jax.nn.softmax(s, axis=-1) * mask


def _shared_attend(q, k, v, mask):
    n, tq, h, d = q.shape
    g = k.shape[2]
    qg = q.reshape(n, tq, g, h // g, d)
    s = jnp.einsum('ntgrd,nsgd->ntgrs', qg, k, precision=HIGHEST) / math.sqrt(d)
    p = _masked_softmax(s, mask[None, :, None, None, :])
    o = jnp.einsum('ntgrs,nsgd->ntgrd', p, v, precision=HIGHEST)
    return o.reshape(n, tq, h, d), p


def _gathered_attend(q, k, v, valid):
    n, tq, h, d = q.shape
    g = k.shape[2]
    qg = q.reshape(n, tq, g, h // g, d)
    s = jnp.einsum('ntgrd,ntgsd->ntgrs', qg, k, precision=HIGHEST) / math.sqrt(d)
    p = _masked_softmax(s, valid[:, :, :, None, :])
    o = jnp.einsum('ntgrs,ntgsd->ntgrd', p, v, precision=HIGHEST)
    return o.reshape(n, tq, h, d)


def _gather_pages(pool, layer, page_table):
    g = pool[layer, page_table]
    return g.reshape((g.shape[0], g.shape[1] * g.shape[2]) + g.shape[3:])


def _to_blocks(rows):
    n, l, g, d = rows.shape
    n_slc = -(-l // SLC_LEN)
    rows = jnp.pad(rows, ((0, 0), (0, n_slc * SLC_LEN - l), (0, 0), (0, 0)))
    return rows.reshape(n, n_slc, SLC_LEN, g, d)


def _cmp_to_slc(n_cmp, n_slc):
    i = np.arange(n_cmp)[:, None] * CMP_STRIDE
    j = np.arange(n_slc)[None, :] * SLC_LEN
    return jnp.asarray((i < j + SLC_LEN) & (i + CMP_LEN > j), dtype=F32)


def _nsa_attend_sample(q_rot, q_raw, gates, q_pos, kc, vc, ks_blk, vs_blk, kw, vw, kw_pos):
    n, tq, h, d = q_rot.shape
    g = kc.shape[2]
    n_cmp, n_slc = kc.shape[1], ks_blk.shape[1]
    cmp_end = jnp.arange(n_cmp) * CMP_STRIDE + (CMP_LEN - 1)
    o_cmp, p_cmp = _shared_attend(q_raw, kc, vc, cmp_end[None, :] <= q_pos[:, None])
    p_slc = jnp.einsum('ntgrc,cj->ntgj', p_cmp, _cmp_to_slc(n_cmp, n_slc), precision=HIGHEST)
    blk = jnp.arange(n_slc)[None, :]
    cur = (q_pos // SLC_LEN)[:, None]
    forced = (blk == 0) | (blk == cur) | (blk == cur - 1)
    avail = blk * SLC_LEN <= q_pos[:, None]
    score = jnp.where(avail[None, :, None, :], p_slc + FORCE_BONUS * forced[None, :, None, :], NEG)
    _, sel = lax.top_k(score, min(SLC_TOPN, n_slc))
    bn = jnp.arange(n)[:, None, None, None]
    gi = jnp.arange(g)[None, None, :, None]
    ks = ks_blk[bn, sel, :, gi].reshape(n, tq, g, -1, d)
    vs = vs_blk[bn, sel, :, gi].reshape(n, tq, g, -1, d)
    sel_pos = (sel[..., None] * SLC_LEN + jnp.arange(SLC_LEN)).reshape(n, tq, g, -1)
    o_slc = _gathered_attend(q_rot, ks, vs, sel_pos <= q_pos[None, :, None, None])
    wmask = ((kw_pos[None, :] <= q_pos[:, None]) & (kw_pos[None, :] > q_pos[:, None] - WINDOW)
             & (kw_pos[None, :] >= 0))
    o_win, _ = _shared_attend(q_rot, kw, vw, wmask)
    gt = jax.nn.sigmoid(gates)
    return gt[..., 0:1] * o_cmp + gt[..., 1:2] * o_slc + gt[..., 2:3] * o_win


def _nsa_sample(q_rot, qb, gb, kc, vc, ks, vs, kw, vw, pos, cache_ck, cache_cv, cache_sk, cache_sv,
                win_k, win_v, layer, page_table, cmp_k, cmp_v):
    cat = lambda pool, new: jnp.concatenate([_gather_pages(pool, layer, page_table), new], axis=1)
    n, tq, g, dh = kc.shape
    past = page_table.shape[1] * PAGE_SIZE
    n_cmp = (past + tq - CMP_LEN) // CMP_STRIDE + 1
    assert (n_cmp + CMP_LEN // CMP_STRIDE - 1) * CMP_STRIDE <= past

    def compress_past(pool, cmp_w):
        return compress_pages(_paged_view(pool, layer, page_table), *cmp_w)[:, :n_cmp].reshape(n, n_cmp, g, dh)

    kcc = compress_past(cache_ck, cmp_k)
    vcc = compress_past(cache_cv, cmp_v)
    ks_blk, vs_blk = _to_blocks(cat(cache_sk, ks)), _to_blocks(cat(cache_sv, vs))
    w, tq = win_k.shape[1], qb.shape[1]
    kw_all = jnp.concatenate([win_k, kw], axis=1)
    vw_all = jnp.concatenate([win_v, vw], axis=1)
    kw_pos = pos[0] - w + jnp.arange(w + tq)
    o = _nsa_attend_sample(q_rot, qb, gb, pos, kcc, vcc, ks_blk, vs_blk, kw_all, vw_all, kw_pos)
    return o, kw_all[:, -w:], vw_all[:, -w:]


def kernel(x_prompt, x_sample, cache_a_k, cache_a_v, cache_a_kidx, cache_b_cmp_k, cache_b_cmp_v, cache_b_slc_k, cache_b_slc_v, state_b_win_k, state_b_win_v, cache_c_k, cache_c_v, page_table, c_prompt, c_sample, w_ada_mix, b_ada_mix, ln_mix_g, ln_mix_b, w_ada_ffn, b_ada_ffn, ln_ffn_g, ln_ffn_b, w_in_even, w_out_even, cmp_pos_k, cmp_w1_k, cmp_w2_k, cmp_pos_v, cmp_w1_v, cmp_w2_v, w_ffn_gate, w_ffn_up, w_ffn_down, w_in_odd, w_out_odd, w_router, b_router, w_moe_gate, w_moe_up, w_moe_down):
    n_p, t_p, d = x_prompt.shape
    n_s, t_s = x_sample.shape[:2]
    past = page_table.shape[1] * PAGE_SIZE
    pos_p = jnp.arange(t_p)
    pos_s = past + jnp.arange(t_s)
    m_p = n_p * t_p
    tm = 512

    n_c = n_p + n_s
    mp = -(-n_c // SUBLANES) * SUBLANES
    c_all = jnp.zeros((mp, d), F32).at[:n_p].set(c_prompt).at[n_p:n_c].set(c_sample)
    ada_mix = ada_all(c_all, w_ada_mix, b_ada_mix)
    ada_ffn = ada_all(c_all, w_ada_ffn, b_ada_ffn)

    assert t_s == 1, "the decode kernels handle one new token per sequence"
    m_s = n_s * t_s

    def mods(ada, layer):
        sh, sc, gt = jnp.split(ada[layer], 3, axis=-1)
        pr = tuple(a[:n_p].reshape(n_p, 1, d) for a in (sh, sc, gt))
        sa = tuple(jnp.repeat(a[n_p:n_c], t_s, axis=0).reshape(1, m_s, d) for a in (sh, sc, gt))
        return pr, sa

    xp = x_prompt.reshape(m_p, d)
    xs = x_sample.reshape(m_s, d)
    tabs = rope_tables(pos_p)
    tabs_s = rope_tables(jnp.tile(pos_s, n_s))

    (sh_p, sc_p, gt_p), (sh_s, sc_s, gt_s) = mods(ada_mix, 0)
    w_even = _even_weight_layout(w_in_even[0])
    p0 = mod_project(xp, sc_p, sh_p, w_even, tabs, EV_ROPE_W, t_p, tm, 256)
    col = lambda c, w: p0[:, c:c + w]
    kc_p, vc_p = col(C_KC, 128).reshape(n_p, t_p, 128), col(C_VC, 128).reshape(n_p, t_p, 128)
    kcc = compress_rows(kc_p, cmp_pos_k[0], cmp_w1_k[0], cmp_w2_k[0])
    vcc = compress_rows(vc_p, cmp_pos_v[0], cmp_w1_v[0], cmp_w2_v[0])
    oa = dsa_prompt(p0, n_p, t_p)
    ob = nsa_prompt(p0, kcc, vcc, n_p, t_p)
    w_out0 = w_out_even[0].astype(BF16)
    xp = outproj_ln(oa, 0, ob, 0, w_out0, xp, gt_p, ln_mix_g[0], ln_mix_b[0], t_p, tm)

    kv4 = lambda a: a.reshape(1, n_p, t_p, 2, HEAD_DIM)
    wlen = min(WINDOW, t_p)
    even_p = dict(
        a_k=kv4(col(C_KA, 128)), a_v=kv4(col(C_VA, 128)), a_kidx=col(C_KI, 64).reshape(1, n_p, t_p, IDX_DIM),
        cmp_k=kv4(kc_p), cmp_v=kv4(vc_p), slc_k=kv4(col(C_KS, 128)), slc_v=kv4(col(C_VS, 128)),
        win_k=kv4(col(C_KW, 128))[:, :, -wlen:], win_v=kv4(col(C_VW, 128))[:, :, -wlen:])

    p0s = mod_project(xs, sc_s, sh_s, w_even, tabs_s, EV_ROPE_W, m_s, m_s, 256)
    cs = lambda c, w, *shape: p0s[:, c:c + w].reshape((n_s, t_s) + shape)
    kvs = lambda c: cs(c, 128, 2, HEAD_DIM)
    ka, va, ki = kvs(C_KA), kvs(C_VA), cs(C_KI, IDX_DIM, IDX_DIM)
    kc, vc, ks, vs, kw, vw = kvs(C_KC), kvs(C_VC), kvs(C_KS), kvs(C_VS), kvs(C_KW), kvs(C_VW)
    oa_s = dsa_sample(p0s, cache_a_k, cache_a_v, cache_a_kidx, 0, page_table)
    cmp_k = (cmp_pos_k[0], cmp_w1_k[0], cmp_w2_k[0])
    cmp_v = (cmp_pos_v[0], cmp_w1_v[0], cmp_w2_v[0])
    ob_s, wk, wv = _nsa_sample(cs(C_QBR, 512, B_HEADS, HEAD_DIM), cs(C_QB, 512, B_HEADS, HEAD_DIM),
                               cs(C_WG + WG_GB_LANE, 3 * B_HEADS, B_HEADS, 3), kc, vc, ks, vs, kw, vw, pos_s,
                               cache_b_cmp_k, cache_b_cmp_v, cache_b_slc_k, cache_b_slc_v,
                               state_b_win_k[0], state_b_win_v[0], 0, page_table, cmp_k, cmp_v)
    xs = outproj_ln(oa_s.reshape(m_s, -1), 0, ob_s.reshape(m_s, -1), 0, w_out0, xs, gt_s,
                    ln_mix_g[0], ln_mix_b[0], m_s, m_s)
    even_s = (ka, va, ki, kc, vc, ks, vs, wk, wv)

    (sh_p, sc_p, gt_p), (sh_s, sc_s, gt_s) = mods(ada_ffn, 0)
    ffn_w = (w_ffn_gate[0].astype(BF16), w_ffn_up[0].astype(BF16), w_ffn_down[0].astype(BF16))
    xp = ffn_ln(xp, sc_p, sh_p, gt_p, *ffn_w, ln_ffn_g[0], ln_ffn_b[0], t_p, tm, D_FF // 2)
    xs = ffn_ln(xs, sc_s, sh_s, gt_s, *ffn_w, ln_ffn_g[0], ln_ffn_b[0], m_s, m_s, D_FF // 2)

    (sh_p, sc_p, gt_p), (sh_s, sc_s, gt_s) = mods(ada_mix, 1)
    no_rope = lambda rows: (jnp.ones((rows, LANES), F32), jnp.zeros((rows, LANES), F32), jnp.zeros((rows, LANES), F32))
    w_odd = w_in_odd[0].astype(BF16)
    w_out1 = w_out_odd[0].astype(BF16)
    p1 = mod_project(xp, sc_p, sh_p, w_odd, no_rope(t_p), 0, t_p, tm, 512)
    osb = sb_prompt(p1, n_p, t_p)
    xp = outproj_ln(osb, 0, osb, 1, w_out1, xp, gt_p, ln_mix_g[1], ln_mix_b[1], t_p, tm)
    hd = C_HEADS * HEAD_DIM
    c_k_p = p1[:, hd:2 * hd].reshape(1, n_p, t_p, C_HEADS, HEAD_DIM)
    c_v_p = p1[:, 2 * hd:].reshape(1, n_p, t_p, C_HEADS, HEAD_DIM)

    p1s = mod_project(xs, sc_s, sh_s, w_odd, no_rope(m_s), 0, m_s, m_s, 512)
    shp = (n_s, t_s, C_HEADS, HEAD_DIM)
    k_s, v_s = p1s[:, hd:2 * hd].reshape(shp), p1s[:, 2 * hd:].reshape(shp)
    osb_s = sb_sample(p1s[:, :hd].reshape(m_s, C_HEADS, HEAD_DIM), cache_c_k, cache_c_v, 0, page_table).reshape(m_s, hd)
    xs = outproj_ln(osb_s, 0, osb_s, 1, w_out1, xs, gt_s, ln_mix_g[1], ln_mix_b[1], m_s, m_s)

    (sh_p, sc_p, gt_p), (sh_s, sc_s, gt_s) = mods(ada_ffn, 1)
    h_p, r_p = router(xp, sc_p, sh_p, w_router[0], b_router[0], t_p, tm)
    h_s, r_s = router(xs, sc_s, sh_s, w_router[0], b_router[0], m_s, m_s)
    tok_of_slot, tile_expert, n_used, slot0, slot1 = moe_routing_tables(jnp.concatenate([r_p, r_s], axis=0))
    ys = moe_expert_rows(jnp.concatenate([h_p, h_s], axis=0), tok_of_slot, tile_expert, n_used,
                         w_moe_gate[0].astype(BF16), w_moe_up[0].astype(BF16), w_moe_down[0].astype(BF16))
    xp = moe_combine_ln(ys, slot0[:m_p], slot1[:m_p], r_p, xp, gt_p, ln_ffn_g[1], ln_ffn_b[1], t_p, MOE_TC)
    xs = moe_combine_ln(ys, slot0[m_p:], slot1[m_p:], r_s, xs, gt_s, ln_ffn_g[1], ln_ffn_b[1], m_s, m_s)

    st = lambda a: a[None]
    (a_k_s, a_v_s, a_kidx_s, b_cmp_k_s, b_cmp_v_s, b_slc_k_s, b_slc_v_s, b_win_k_s, b_win_v_s) = [st(a) for a in even_s]
    return (xp.reshape(n_p, t_p, d), xs.reshape(n_s, t_s, d),
            even_p['a_k'], a_k_s, even_p['a_v'], a_v_s, even_p['a_kidx'], a_kidx_s,
            even_p['cmp_k'], b_cmp_k_s, even_p['cmp_v'], b_cmp_v_s,
            even_p['slc_k'], b_slc_k_s, even_p['slc_v'], b_slc_v_s,
            even_p['win_k'], b_win_k_s, even_p['win_v'], b_win_v_s,
            c_k_p, st(k_s), c_v_p, st(v_s))
```

```python
import functools
import math

import numpy as np
import jax
import jax.numpy as jnp
from jax import lax
from jax.experimental import pallas as pl
from jax.experimental.pallas import tpu as pltpu

D_MODEL = 1024
PAGE_SIZE = 128
HEAD_DIM = 64
ROPE_DIM = HEAD_DIM // 4
ROPE_THETA = 500000.0
Q_BLOCK = 128
A_HEADS = 8
A_KV_HEADS = 2
IDX_HEADS = 8
IDX_DIM = 64
IDX_TOPK = 256
B_HEADS = 8
B_KV_HEADS = 2
CMP_LEN = 32
CMP_STRIDE = 16
CMP_HID = 128
SLC_LEN = 64
SLC_TOPN = 16
WINDOW = 512
FORCE_BONUS = 100.0
C_HEADS = 16
D_FF = 2816
N_EXPERTS = 8
TOP_K = 2
D_FF_EXPERT = 3584
DEPTH = 2
ALPHA = (2 * DEPTH) ** 0.25
LN_EPS = 1e-5
NEG = -1e30
EVEN_SPLIT = (A_HEADS * HEAD_DIM, A_KV_HEADS * HEAD_DIM, A_KV_HEADS * HEAD_DIM,
              IDX_HEADS * IDX_DIM, IDX_DIM, IDX_HEADS,
              B_HEADS * HEAD_DIM, 6 * B_KV_HEADS * HEAD_DIM, 3 * B_HEADS)
EVEN_OFFSETS = tuple(int(v) for v in np.cumsum(EVEN_SPLIT)[:-1])

LANES = 128
SUBLANES = 8
VMEM_LIMIT_BYTES = 56 * 1024 * 1024

F32 = jnp.float32
BF16 = jnp.bfloat16
I32 = jnp.int32
INT_MIN = -2 ** 31
HIGHEST = lax.Precision.HIGHEST

C_QA, C_QI, C_QBR, C_KA, C_KS, C_KW, C_KI = 0, 512, 1024, 1536, 1664, 1792, 1920
EV_ROPE_W = 2048
C_QB, C_VA, C_KC, C_VC, C_VS, C_VW, C_WG = 2048, 2560, 2688, 2816, 2944, 3072, 3200
EV_W = 3328
WG_GB_LANE = IDX_HEADS


def _cparams(sem):
    return pltpu.CompilerParams(dimension_semantics=sem, vmem_limit_bytes=VMEM_LIMIT_BYTES)


def _sortable_key(x):
    b = lax.bitcast_convert_type(x + 0.0, I32)
    return jnp.where(b < 0, b ^ jnp.int32(0x7FFFFFFF), b)


_KEY_NEG = int(np.array(NEG, np.float32).view(np.int32)) ^ 0x7FFFFFFF


def _layer_norm_rows(y, g, b):
    mu = jnp.mean(y, axis=-1, keepdims=True)
    d = y - mu
    var = jnp.mean(d * d, axis=-1, keepdims=True)
    return d * lax.rsqrt(var + LN_EPS) * g + b


def _silu(x):
    return x * jax.nn.sigmoid(x)


def _dot_t(a, b):
    return lax.dot_general(a, b, (((1,), (1,)), ((), ())), preferred_element_type=F32)


def _dot(a, b):
    return jnp.dot(a, b, preferred_element_type=F32)


def _mm(a, w):
    if w.dtype == BF16:
        return jnp.dot(a.astype(BF16), w, preferred_element_type=F32)
    return jnp.dot(a, w, precision=HIGHEST, preferred_element_type=F32)


def _topk_bounds(load, n_dyn, n_total, tail_key, k, rows, cw, idx_bits):
    lane = lax.broadcasted_iota(I32, (rows, cw), 1)
    n_tail = ((n_total - n_dyn) * cw).astype(F32) if not isinstance(n_dyn, int) else float((n_total - n_dyn) * cw)
    dyn_w = n_dyn * cw
    kf = float(k)

    def count(pred):
        def body(c, tot):
            hit = pred(load(c), lane + c * cw).astype(F32)
            for t in range(cw // LANES):
                tot = tot + hit[:, t * LANES:(t + 1) * LANES]
            return tot
        tot = lax.fori_loop(0, n_dyn, body, jnp.zeros((rows, LANES), F32))
        return jnp.sum(tot, axis=1, keepdims=True)

    def bit_body(it, u):
        bit = lax.shift_left(jnp.int32(1), jnp.int32(31) - it)
        cu = u | bit
        cand = cu ^ jnp.int32(INT_MIN)
        cnt = count(lambda keys, idx: keys >= cand) + n_tail * (jnp.int32(tail_key) >= cand).astype(F32)
        return jnp.where(cnt >= kf, cu, u)

    u = lax.fori_loop(0, 32, bit_body, jnp.zeros((rows, 1), I32))
    thr = u ^ jnp.int32(INT_MIN)
    tail_gt = (jnp.int32(tail_key) > thr).astype(F32)
    tail_eq = (jnp.int32(tail_key) == thr).astype(F32)
    c_gt = count(lambda keys, idx: keys > thr) + n_tail * tail_gt
    need = kf - c_gt

    def j_body(it, j0):
        bit = lax.shift_left(jnp.int32(1), jnp.int32(idx_bits - 1) - it)
        cj = j0 | bit
        f = count(lambda keys, idx: (keys == thr) & (idx < cj))
        f = f + tail_eq * jnp.clip((cj - dyn_w).astype(F32), 0.0, n_tail)
        return jnp.where(f < need, cj, j0)

    c_eq = count(lambda keys, idx: keys == thr) + n_tail * tail_eq
    j0 = lax.cond(jnp.max(c_eq - need) > 0.0,
                  lambda: lax.fori_loop(0, idx_bits, j_body, jnp.zeros((rows, 1), I32)),
                  lambda: jnp.full((rows, 1), 2 ** idx_bits - 1, I32))
    return thr, j0


def _ada_kernel(c_ref, w_ref, b_ref, o_ref):
    s = _silu(c_ref[...])
    o_ref[0] = jnp.dot(s, w_ref[0], precision=HIGHEST, preferred_element_type=F32) + b_ref[0]


def ada_all(c_all, w, b):
    mp, d = c_all.shape
    nl, _, n3 = w.shape
    tn = 512
    return pl.pallas_call(
        _ada_kernel,
        grid=(nl, n3 // tn),
        in_specs=[pl.BlockSpec((mp, d), lambda l, j: (0, 0)),
                  pl.BlockSpec((1, d, tn), lambda l, j: (l, 0, j)),
                  pl.BlockSpec((1, 1, tn), lambda l, j: (l, 0, j))],
        out_specs=pl.BlockSpec((1, mp, tn), lambda l, j: (l, 0, j)),
        out_shape=jax.ShapeDtypeStruct((nl, mp, n3), F32),
        compiler_params=_cparams(("arbitrary", "arbitrary")),
        name="ada",
    )(c_all, w, b.reshape(nl, 1, n3))


def _proj_kernel(x_ref, sc_ref, sh_ref, w_ref, cos_ref, sn_ref, sp_ref, o_ref, *, n_rope_tiles, tn):
    j = pl.program_id(1)
    acc = _mm(x_ref[...] * (1.0 + sc_ref[0]) + sh_ref[0], w_ref[...])

    @pl.when(j < n_rope_tiles)
    def _():
        c, sn, sp = cos_ref[...], sn_ref[...], sp_ref[...]
        for k in range(tn // LANES):
            seg = acc[:, k * LANES:(k + 1) * LANES]
            o_ref[:, k * LANES:(k + 1) * LANES] = (
                seg * c + pltpu.roll(seg, LANES - ROPE_DIM // 2, 1) * sn + pltpu.roll(seg, ROPE_DIM // 2, 1) * sp)

    @pl.when(j >= n_rope_tiles)
    def _():
        o_ref[...] = acc


def mod_project(x, sc, sh, w, rope_tabs, rope_width, seq, tm, tn):
    m, d = x.shape
    n = w.shape[1]
    cos_t, sn_t, sp_t = rope_tabs
    tiles_per_seq = seq // tm
    mod_spec = pl.BlockSpec((1, sc.shape[1], d), lambda i, j: (i // tiles_per_seq, 0, 0))
    tab_spec = pl.BlockSpec((tm, LANES), lambda i, j: (i % tiles_per_seq, 0))
    return pl.pallas_call(
        functools.partial(_proj_kernel, n_rope_tiles=rope_width // tn, tn=tn),
        grid=(m // tm, n // tn),
        in_specs=[pl.BlockSpec((tm, d), lambda i, j: (i, 0)), mod_spec, mod_spec,
                  pl.BlockSpec((d, tn), lambda i, j: (0, j)), tab_spec, tab_spec, tab_spec],
        out_specs=pl.BlockSpec((tm, tn), lambda i, j: (i, j)),
        out_shape=jax.ShapeDtypeStruct((m, n), F32),
        compiler_params=_cparams(("parallel", "arbitrary")),
        name="mod_project",
    )(x, sc, sh, w, cos_t, sn_t, sp_t)


def _outproj_ln_kernel(a1_ref, a2_ref, w_ref, x_ref, gt_ref, g_ref, b_ref, o_ref):
    a = jnp.concatenate([a1_ref[...], a2_ref[...]], axis=1)
    y = ALPHA * x_ref[...] + gt_ref[0] * _mm(a, w_ref[...])
    o_ref[...] = _layer_norm_rows(y, g_ref[...], b_ref[...])


def outproj_ln(a1, a1_blk, a2, a2_blk, w, x, gt, ln_g, ln_b, seq, tm):
    m, d = x.shape
    half = d // 2
    tiles_per_seq = seq // tm
    return pl.pallas_call(
        _outproj_ln_kernel,
        grid=(m // tm,),
        in_specs=[pl.BlockSpec((tm, half), lambda i: (i, a1_blk)),
                  pl.BlockSpec((tm, half), lambda i: (i, a2_blk)),
                  pl.BlockSpec((d, d), lambda i: (0, 0)),
                  pl.BlockSpec((tm, d), lambda i: (i, 0)),
                  pl.BlockSpec((1, gt.shape[1], d), lambda i: (i // tiles_per_seq, 0, 0)),
                  pl.BlockSpec((1, d), lambda i: (0, 0)),
                  pl.BlockSpec((1, d), lambda i: (0, 0))],
        out_specs=pl.BlockSpec((tm, d), lambda i: (i, 0)),
        out_shape=jax.ShapeDtypeStruct((m, d), F32),
        compiler_params=_cparams(("parallel",)),
        name="outproj_ln",
    )(a1, a2, w, x, gt, ln_g.reshape(1, d), ln_b.reshape(1, d))


def _ffn_ln_kernel(x_ref, sc_ref, sh_ref, gt_ref, wg_ref, wu_ref, wd_ref, g_ref, b_ref, o_ref, h_scr, acc_scr):
    j = pl.program_id(1)

    @pl.when(j == 0)
    def _():
        h_scr[...] = (x_ref[...] * (1.0 + sc_ref[0]) + sh_ref[0]).astype(h_scr.dtype)
        acc_scr[...] = jnp.zeros_like(acc_scr)

    h = h_scr[...]
    acc_scr[...] += _mm(_silu(_mm(h, wg_ref[...])) * _mm(h, wu_ref[...]), wd_ref[...])

    @pl.when(j == pl.num_programs(1) - 1)
    def _():
        y = ALPHA * x_ref[...] + gt_ref[0] * acc_scr[...]
        o_ref[...] = _layer_norm_rows(y, g_ref[...], b_ref[...])


def ffn_ln(x, sc, sh, gt, wg, wu, wd, ln_g, ln_b, seq, tm, tf):
    m, d = x.shape
    f = wg.shape[1]
    tiles_per_seq = seq // tm
    mod_spec = pl.BlockSpec((1, sc.shape[1], d), lambda i, j: (i // tiles_per_seq, 0, 0))
    vec_spec = pl.BlockSpec((1, d), lambda i, j: (0, 0))
    return pl.pallas_call(
        _ffn_ln_kernel,
        grid=(m // tm, f // tf),
        in_specs=[pl.BlockSpec((tm, d), lambda i, j: (i, 0)), mod_spec, mod_spec, mod_spec,
                  pl.BlockSpec((d, tf), lambda i, j: (0, j)),
                  pl.BlockSpec((d, tf), lambda i, j: (0, j)),
                  pl.BlockSpec((tf, d), lambda i, j: (j, 0)), vec_spec, vec_spec],
        out_specs=pl.BlockSpec((tm, d), lambda i, j: (i, 0)),
        out_shape=jax.ShapeDtypeStruct((m, d), F32),
        scratch_shapes=[pltpu.VMEM((tm, d), wg.dtype), pltpu.VMEM((tm, d), F32)],
        compiler_params=_cparams(("parallel", "arbitrary")),
        name="ffn_ln",
    )(x, sc, sh, gt, wg, wu, wd, ln_g.reshape(1, d), ln_b.reshape(1, d))


def _head_tile(x, h, dst):
    t = x[:, (h // 2) * LANES:(h // 2 + 1) * LANES]
    if h % 2 != dst:
        t = pltpu.roll(t, HEAD_DIM, 1)
    half = lax.broadcasted_iota(I32, t.shape, 1) // HEAD_DIM
    return jnp.where(half == dst, t, 0.0)


def _stack_group(x, g, heads_per_group, scale):
    tiles = [_head_tile(x, g * heads_per_group + r, g) * scale for r in range(heads_per_group)]
    return jnp.concatenate(tiles, axis=0).astype(BF16)


def _unstack_groups(o_groups, heads_per_group, q):
    n_heads = len(o_groups) * heads_per_group
    lane_half = lax.broadcasted_iota(I32, (q, LANES), 1) // HEAD_DIM
    pairs = []
    for hp in range(n_heads // 2):
        halves = []
        for pos in range(2):
            h = 2 * hp + pos
            g, r = divmod(h, heads_per_group)
            t = o_groups[g][r * q:(r + 1) * q]
            if g != pos:
                t = pltpu.roll(t, HEAD_DIM, 1)
            halves.append(t)
        pairs.append(jnp.where(lane_half == 0, halves[0], halves[1]))
    return jnp.concatenate(pairs, axis=1)


def _mask_bias(mk):
    return (mk - 1.0) * (-NEG)


def _flash_step(qg, kc, vc, bias, carry):
    m, l, acc = carry
    s = _dot_t(qg, kc) + bias
    m_new = jnp.maximum(m, jnp.max(s, axis=1, keepdims=True))
    alpha = jnp.exp(m - m_new)
    p = jnp.exp(s - m_new)
    l = alpha * l + jnp.sum(p, axis=1, keepdims=True)
    acc = alpha * acc + _dot(p.astype(BF16), vc)
    return m_new, l, acc


def _flash_init(rows):
    return (jnp.full((rows, 1), NEG, F32), jnp.zeros((rows, 1), F32), jnp.zeros((rows, LANES), F32))


def _flash_out(carry):
    _, l, acc = carry
    return acc / jnp.maximum(l, 1e-30)


DSA_CW = 512


def _dsa_kernel(qa_ref, qi_ref, wg_ref, ki_ref, ka_ref, va_ref, o_ref, key_scr, msk_scr, *, seq, topk):
    q = Q_BLOCK
    i = pl.program_id(1)
    q0 = i * q
    nch = seq // DSA_CW
    n_dyn = (q0 + q + DSA_CW - 1) // DSA_CW
    row = q0 + lax.broadcasted_iota(I32, (q, 1), 0)
    lane = lax.broadcasted_iota(I32, (q, DSA_CW), 1)

    qi = qi_ref[...]
    wi = wg_ref[...]
    qih = []
    for h in range(IDX_HEADS):
        t = qi[:, (h // 2) * LANES:(h // 2 + 1) * LANES]
        if h % 2:
            t = pltpu.roll(t, HEAD_DIM, 1)
        qih.append(t.astype(BF16))

    def score_body(c, _):
        off = pl.multiple_of(c * DSA_CW, DSA_CW)
        kc = ki_ref[pl.ds(off, DSA_CW), :].astype(BF16)
        acc = jnp.zeros((q, DSA_CW), F32)
        for h in range(IDX_HEADS):
            acc = acc + wi[:, h:h + 1] * jnp.maximum(_dot_t(qih[h], kc), 0.0)
        acc = jnp.where(lane + off <= row, acc, NEG)
        key_scr[c] = _sortable_key(acc)
        return 0

    lax.fori_loop(0, n_dyn, score_body, 0)

    thr, j0 = _topk_bounds(lambda c: key_scr[c], n_dyn, nch, _KEY_NEG, topk, q, DSA_CW,
                           int(math.log2(seq)))

    def mask_body(c, _):
        keys = key_scr[c]
        idx = lane + c * DSA_CW
        sel = (keys > thr) | ((keys == thr) & (idx <= j0))
        msk_scr[c] = _mask_bias((sel & (idx <= row)).astype(F32))
        return 0

    lax.fori_loop(0, n_dyn, mask_body, 0)

    qa = qa_ref[...]
    hpg = A_HEADS // A_KV_HEADS
    qgs = [_stack_group(qa, g, hpg, 1.0 / math.sqrt(HEAD_DIM)) for g in range(A_KV_HEADS)]

    def attn_body(c, carries):
        off = pl.multiple_of(c * DSA_CW, DSA_CW)
        kc = ka_ref[pl.ds(off, DSA_CW), :].astype(BF16)
        vc = va_ref[pl.ds(off, DSA_CW), :].astype(BF16)
        bias = jnp.concatenate([msk_scr[c]] * hpg, axis=0)
        return tuple(_flash_step(qgs[g], kc, vc, bias, carries[g]) for g in range(A_KV_HEADS))

    carries = lax.fori_loop(0, n_dyn, attn_body, tuple(_flash_init(hpg * q) for _ in range(A_KV_HEADS)))
    o_ref[...] = _unstack_groups([_flash_out(c) for c in carries], hpg, q)


def dsa_prompt(p, n_seq, seq):
    nq = seq // Q_BLOCK
    topk = min(IDX_TOPK, seq // 4)
    qspec = lambda cb: pl.BlockSpec((Q_BLOCK, 512), lambda n, i: (n * nq + i, cb))
    kvspec = lambda cb: pl.BlockSpec((seq, LANES), lambda n, i: (n, cb))
    return pl.pallas_call(
        functools.partial(_dsa_kernel, seq=seq, topk=topk),
        grid=(n_seq, nq),
        in_specs=[qspec(C_QA // 512), qspec(C_QI // 512),
                  pl.BlockSpec((Q_BLOCK, LANES), lambda n, i: (n * nq + i, C_WG // LANES)),
                  kvspec(C_KI // LANES), kvspec(C_KA // LANES), kvspec(C_VA // LANES)],
        out_specs=pl.BlockSpec((Q_BLOCK, 512), lambda n, i: (n * nq + i, 0)),
        out_shape=jax.ShapeDtypeStruct((n_seq * seq, 512), F32),
        scratch_shapes=[pltpu.VMEM((seq // DSA_CW, Q_BLOCK, DSA_CW), I32),
                        pltpu.VMEM((seq // DSA_CW, Q_BLOCK, DSA_CW), F32)],
        compiler_params=_cparams(("parallel", "arbitrary")),
        name="dsa_prompt",
    )(p, p, p, p, p, p)


def _compress_kernel(x_ref, pa_ref, pb_ref, w1a_ref, w1b_ref, w2_ref, o_ref):
    x = x_ref[0]
    nchunk = x.shape[0]
    a = _dot((x + pa_ref[...]).astype(BF16), w1a_ref[...])
    b = _dot((x + pb_ref[...]).astype(BF16), w1b_ref[...])
    h = a + pltpu.roll(b, nchunk - 1, 0)
    o_ref[0] = _dot(_silu(h).astype(BF16), w2_ref[...])


def _compress_operands(pos_emb, w1, w2, g):
    gd = g * HEAD_DIM
    r = CMP_LEN // CMP_STRIDE
    w1c = w1.reshape(r, CMP_STRIDE, HEAD_DIM, CMP_HID)
    eye = jnp.eye(g, dtype=F32)
    big = [jnp.einsum('ldh,gk->lgdkh', w1c[j], eye).reshape(CMP_STRIDE * gd, g * CMP_HID).astype(BF16) for j in range(r)]
    w2big = jnp.einsum('hd,gk->ghkd', w2, eye).reshape(g * CMP_HID, gd).astype(BF16)
    pe = pos_emb.reshape(r, CMP_STRIDE, 1, HEAD_DIM)
    pab = [jnp.broadcast_to(pe[j], (CMP_STRIDE, g, HEAD_DIM)).reshape(1, CMP_STRIDE * gd) for j in range(r)]
    return pab, big, w2big


def _compress_pages_kernel(x_ref, pa_ref, pb_ref, w1a_ref, w1b_ref, w2_ref, o_ref, xs):
    npg, gd, ps = x_ref.shape[1:]
    nchunk = npg * ps // CMP_STRIDE

    def tr(p, c):
        xs[pl.ds(pl.multiple_of(p * ps, ps), ps), :] = x_ref[0, p].T
        return c

    lax.fori_loop(0, npg, tr, 0)
    a = jnp.zeros((nchunk, w1a_ref.shape[1]), F32)
    b = jnp.zeros((nchunk, w1a_ref.shape[1]), F32)
    for l in range(CMP_STRIDE):
        rows = xs[pl.ds(l, nchunk, stride=CMP_STRIDE), :]
        sl = slice(l * gd, (l + 1) * gd)
        a = a + _dot((rows + pa_ref[:, sl]).astype(BF16), w1a_ref[sl, :])
        b = b + _dot((rows + pb_ref[:, sl]).astype(BF16), w1b_ref[sl, :])
    h = a + pltpu.roll(b, nchunk - 1, 0)
    o_ref[0] = _dot(_silu(h).astype(BF16), w2_ref[...])


def compress_pages(pages, pos_emb, w1, w2):
    n, npg, gd, ps = pages.shape
    g = gd // HEAD_DIM
    nchunk = npg * ps // CMP_STRIDE
    pab, big, w2big = _compress_operands(pos_emb, w1, w2, g)
    k2 = CMP_STRIDE * gd
    return pl.pallas_call(
        _compress_pages_kernel,
        grid=(n,),
        in_specs=[pl.BlockSpec((1, npg, gd, ps), lambda i: (i, 0, 0, 0)),
                  pl.BlockSpec((1, k2), lambda i: (0, 0)), pl.BlockSpec((1, k2), lambda i: (0, 0)),
                  pl.BlockSpec((k2, g * CMP_HID), lambda i: (0, 0)),
                  pl.BlockSpec((k2, g * CMP_HID), lambda i: (0, 0)),
                  pl.BlockSpec((g * CMP_HID, gd), lambda i: (0, 0))],
        out_specs=pl.BlockSpec((1, nchunk, gd), lambda i: (i, 0, 0)),
        out_shape=jax.ShapeDtypeStruct((n, nchunk, gd), F32),
        scratch_shapes=[pltpu.VMEM((npg * ps, gd), F32)],
        compiler_params=_cparams(("parallel",)),
        name="compress_pages",
    )(pages, pab[0], pab[1], big[0], big[1], w2big)


def compress_rows(rows, pos_emb, w1, w2):
    n, l, gd = rows.shape
    g = gd // HEAD_DIM
    nchunk = l // CMP_STRIDE
    x = rows.reshape(n, nchunk, CMP_STRIDE * gd)
    pab, big, w2big = _compress_operands(pos_emb, w1, w2, g)
    k2 = CMP_STRIDE * gd
    return pl.pallas_call(
        _compress_kernel,
        grid=(n,),
        in_specs=[pl.BlockSpec((1, nchunk, k2), lambda i: (i, 0, 0)),
                  pl.BlockSpec((1, k2), lambda i: (0, 0)), pl.BlockSpec((1, k2), lambda i: (0, 0)),
                  pl.BlockSpec((k2, g * CMP_HID), lambda i: (0, 0)),
                  pl.BlockSpec((k2, g * CMP_HID), lambda i: (0, 0)),
                  pl.BlockSpec((g * CMP_HID, gd), lambda i: (0, 0))],
        out_specs=pl.BlockSpec((1, nchunk, gd), lambda i: (i, 0, 0)),
        out_shape=jax.ShapeDtypeStruct((n, nchunk, gd), F32),
        compiler_params=_cparams(("parallel",)),
        name="compress",
    )(x, pab[0], pab[1], big[0], big[1], w2big)


NSA_CW = 512


def _nsa_kernel(qb_ref, qr_ref, wg_ref, kcc_ref, vcc_ref, ks_ref, vs_ref, kw_ref, vw_ref, o_ref, *, seq):
    q = Q_BLOCK
    i = pl.program_id(1)
    q0 = i * q
    hpg = B_HEADS // B_KV_HEADS
    ncmp = seq // CMP_STRIDE
    nslc = seq // SLC_LEN
    n_dyn = (q0 + q + NSA_CW - 1) // NSA_CW
    scale = 1.0 / math.sqrt(HEAD_DIM)
    row = q0 + lax.broadcasted_iota(I32, (q, 1), 0)

    qb = qb_ref[...]
    qr = qr_ref[...]
    kcc = kcc_ref[0].astype(BF16)
    vcc = vcc_ref[0].astype(BF16)

    cidx = lax.broadcasted_iota(I32, (q, ncmp), 1)
    cmask = (cidx * CMP_STRIDE + (CMP_LEN - 1) <= row).astype(F32)
    cmask_g = jnp.concatenate([cmask] * hpg, axis=0)
    assert nslc <= LANES
    ci = lax.broadcasted_iota(I32, (ncmp, LANES), 0) * CMP_STRIDE
    sj = lax.broadcasted_iota(I32, (ncmp, LANES), 1) * SLC_LEN
    cmp2slc = ((ci < sj + SLC_LEN) & (ci + CMP_LEN > sj) & (sj < seq)).astype(BF16)
    blk = lax.broadcasted_iota(I32, (q, LANES), 1)
    cur = row // SLC_LEN
    forced = ((blk == 0) | (blk == cur) | (blk == cur - 1)).astype(F32)
    avail = (blk * SLC_LEN <= row) & (blk < nslc)

    o_cmp, sel_keys = [], []
    for g in range(B_KV_HEADS):
        qg = _stack_group(qb, g, hpg, scale)
        s = jnp.where(cmask_g > 0.5, _dot_t(qg, kcc), NEG)
        e = jnp.exp(s - jnp.max(s, axis=1, keepdims=True)) * cmask_g
        p = e / jnp.maximum(jnp.sum(e, axis=1, keepdims=True), 1e-30)
        o_cmp.append(_dot(p.astype(BF16), vcc))
        psum = p[0:q]
        for r in range(1, hpg):
            psum = psum + p[r * q:(r + 1) * q]
        p_hi = psum.astype(BF16)
        p_lo = (psum - p_hi.astype(F32)).astype(BF16)
        p_slc = _dot(p_hi, cmp2slc) + _dot(p_lo, cmp2slc)
        score = jnp.where(avail, p_slc + FORCE_BONUS * forced, NEG)
        sel_keys.append(jnp.where(blk < nslc, _sortable_key(score), jnp.int32(INT_MIN)))

    keys = jnp.concatenate(sel_keys, axis=0)
    ntop = min(SLC_TOPN, nslc)
    thr, j0 = _topk_bounds(lambda c: keys, 1, 1, INT_MIN, ntop, B_KV_HEADS * q, LANES, int(math.log2(LANES)))
    lane_b = lax.broadcasted_iota(I32, keys.shape, 1)
    sel = ((keys > thr) | ((keys == thr) & (lane_b <= j0))).astype(BF16)

    lane = lax.broadcasted_iota(I32, (q, NSA_CW), 1)
    eb = lax.broadcasted_iota(I32, (LANES, NSA_CW), 0)
    es = lax.broadcasted_iota(I32, (LANES, NSA_CW), 1)

    qgs = [_stack_group(qr, g, hpg, scale) for g in range(B_KV_HEADS)]
    init = tuple(_flash_init(hpg * q) for _ in range(B_KV_HEADS))

    def slc_body(c, carries):
        off = pl.multiple_of(c * NSA_CW, NSA_CW)
        kc = ks_ref[pl.ds(off, NSA_CW), :].astype(BF16)
        vc = vs_ref[pl.ds(off, NSA_CW), :].astype(BF16)
        expand = ((es + off) // SLC_LEN == eb).astype(BF16)
        causal = (lane + off <= row).astype(F32)
        out = []
        for g in range(B_KV_HEADS):
            bias = _mask_bias(_dot(sel[g * q:(g + 1) * q], expand) * causal)
            out.append(_flash_step(qgs[g], kc, vc, jnp.concatenate([bias] * hpg, axis=0), carries[g]))
        return tuple(out)

    o_slc = [_flash_out(c) for c in lax.fori_loop(0, n_dyn, slc_body, init)]

    lane_w = lax.broadcasted_iota(I32, (q, q), 1)

    def win_body(kb, carries):
        off = pl.multiple_of(kb * q, q)
        kc = kw_ref[pl.ds(off, q), :].astype(BF16)
        vc = vw_ref[pl.ds(off, q), :].astype(BF16)
        col = lane_w + off
        bias = _mask_bias(((col <= row) & (col > row - WINDOW)).astype(F32))
        bias = jnp.concatenate([bias] * hpg, axis=0)
        return tuple(_flash_step(qgs[g], kc, vc, bias, carries[g]) for g in range(B_KV_HEADS))

    kb_lo = jnp.maximum(i - WINDOW // q, 0)
    o_win = [_flash_out(c) for c in lax.fori_loop(kb_lo, i + 1, win_body, init)]

    gate = jax.nn.sigmoid(wg_ref[...])
    outs = []
    for g in range(B_KV_HEADS):
        cols = [[gate[:, WG_GB_LANE + (g * hpg + r) * 3 + b: WG_GB_LANE + (g * hpg + r) * 3 + b + 1]
                 for r in range(hpg)] for b in range(3)]
        gc = [jnp.concatenate(cols[b], axis=0) for b in range(3)]
        outs.append(gc[0] * o_cmp[g] + gc[1] * o_slc[g] + gc[2] * o_win[g])
    o_ref[...] = _unstack_groups(outs, hpg, q)


def nsa_prompt(p, kcc, vcc, n_seq, seq):
    nq = seq // Q_BLOCK
    ncmp = seq // CMP_STRIDE
    qspec = lambda cb: pl.BlockSpec((Q_BLOCK, 512), lambda n, i: (n * nq + i, cb))
    kvspec = lambda cb: pl.BlockSpec((seq, LANES), lambda n, i: (n, cb))
    cspec = pl.BlockSpec((1, ncmp, LANES), lambda n, i: (n, 0, 0))
    return pl.pallas_call(
        functools.partial(_nsa_kernel, seq=seq),
        grid=(n_seq, nq),
        in_specs=[qspec(C_QB // 512), qspec(C_QBR // 512),
                  pl.BlockSpec((Q_BLOCK, LANES), lambda n, i: (n * nq + i, C_WG // LANES)),
                  cspec, cspec,
                  kvspec(C_KS // LANES), kvspec(C_VS // LANES), kvspec(C_KW // LANES), kvspec(C_VW // LANES)],
        out_specs=pl.BlockSpec((Q_BLOCK, 512), lambda n, i: (n * nq + i, 0)),
        out_shape=jax.ShapeDtypeStruct((n_seq * seq, 512), F32),
        compiler_params=_cparams(("parallel", "arbitrary")),
        name="nsa_prompt",
    )(p, p, p, kcc, vcc, p, p, p, p)


SB_DEAD = -110.0
SB_PAIRS = 4


def _log_sigmoid_pair(z):
    ls = jnp.minimum(z, 0.0) - jnp.log(1.0 + jnp.exp(-jnp.abs(z)))
    return ls, ls - z


def _split_bf16(x):
    hi = x.astype(BF16)
    return hi, (x - hi.astype(F32)).astype(BF16)


def _sb_kernel(q_ref, k_ref, v_ref, o_ref):
    q = Q_BLOCK
    i = pl.program_id(2)
    q0 = i * q
    nh = 2 * SB_PAIRS
    row = q0 + (lax.broadcasted_iota(I32, (nh * q, 1), 0) & (q - 1))
    lane = lax.broadcasted_iota(I32, (nh * q, q), 1)
    tri = (lax.broadcasted_iota(I32, (q, q), 0) > lax.broadcasted_iota(I32, (q, q), 1)).astype(BF16)
    half = lax.broadcasted_iota(I32, (q, LANES), 1) // HEAD_DIM
    qq = q_ref[...] * (1.0 / math.sqrt(HEAD_DIM))
    q_cat = []
    for hp in range(SB_PAIRS):
        for h in range(2):
            q_hi, q_lo = _split_bf16(jnp.where(half == h, qq[:, hp * LANES:(hp + 1) * LANES], 0.0))
            q_cat.append(jnp.concatenate([q_hi, q_hi, q_lo], axis=1))

    def cond(st):
        return (st[0] >= 0) & (jnp.max(st[1]) > SB_DEAD)

    def body(st):
        j, carry, accs = st[0], st[1], list(st[2:])
        off = pl.multiple_of(j * q, q)
        mask = lane + off < row
        zs, vbs = [], []
        for hp in range(SB_PAIRS):
            k_hi, k_lo = _split_bf16(k_ref[pl.ds(off, q), hp * LANES:(hp + 1) * LANES])
            k_cat = jnp.concatenate([k_hi, k_lo, k_hi], axis=1)
            vbs.append(v_ref[pl.ds(off, q), hp * LANES:(hp + 1) * LANES].astype(BF16))
            for h in range(2):
                zs.append(_dot_t(q_cat[2 * hp + h], k_cat))
        ls, lsn = _log_sigmoid_pair(jnp.concatenate(zs, axis=0))
        u = jnp.where(mask, lsn, 0.0)
        u_hi, u_lo = _split_bf16(u)
        later = _dot(u_hi, tri) + _dot(u_lo, tri) + carry
        a = jnp.where(mask, jnp.exp(ls + later), 0.0).astype(BF16)
        for hp in range(SB_PAIRS):
            for h in range(2):
                r0 = (2 * hp + h) * q
                accs[hp] = accs[hp] + jnp.where(half == h, _dot(a[r0:r0 + q], vbs[hp]), 0.0)
        return (j - 1, carry + jnp.sum(u, axis=1, keepdims=True), *accs)

    zl = jnp.zeros((q, LANES), F32)
    st = lax.while_loop(cond, body, (i, jnp.zeros((nh * q, 1), F32)) + (zl,) * SB_PAIRS)
    o_ref[...] = jnp.concatenate(st[2:], axis=1)


def sb_prompt(p1, n_seq, seq):
    nq = seq // Q_BLOCK
    width = SB_PAIRS * LANES
    ngrp = C_HEADS * HEAD_DIM // width
    return pl.pallas_call(
        _sb_kernel,
        grid=(n_seq, ngrp, nq),
        in_specs=[pl.BlockSpec((Q_BLOCK, width), lambda n, hg, i: (n * nq + i, hg)),
                  pl.BlockSpec((seq, width), lambda n, hg, i: (n, ngrp + hg)),
                  pl.BlockSpec((seq, width), lambda n, hg, i: (n, 2 * ngrp + hg))],
        out_specs=pl.BlockSpec((Q_BLOCK, width), lambda n, hg, i: (n * nq + i, hg)),
        out_shape=jax.ShapeDtypeStruct((n_seq * seq, C_HEADS * HEAD_DIM), F32),
        compiler_params=_cparams(("parallel", "parallel", "arbitrary")),
        name="sb_prompt",
    )(p1, p1, p1)


def _sb_sample_kernel(pt_ref, q_ref, ck_hbm, cv_hbm, o_ref, kbuf, vbuf, sem, *, layer, n_pages):
    n = pl.program_id(0)
    q3 = q_ref[0] * (1.0 / math.sqrt(HEAD_DIM))
    nh = q_ref.shape[1]
    lane = lax.broadcasted_iota(I32, (nh, PAGE_SIZE), 1)

    def cond(st):
        p, carry, _ = st
        return (p >= 0) & (jnp.max(carry) > SB_DEAD)

    def body(st):
        p, carry, acc = st
        phys = pt_ref[n, p]
        ck = pltpu.make_async_copy(ck_hbm.at[layer, phys], kbuf, sem.at[0])
        cv = pltpu.make_async_copy(cv_hbm.at[layer, phys], vbuf, sem.at[1])
        ck.start()
        cv.start()
        ck.wait()
        cv.wait()
        z = jnp.sum(kbuf[...] * q3, axis=1)
        ls, u = _log_sigmoid_pair(z)
        y = u
        k = 1
        while k < PAGE_SIZE:
            y = y + jnp.where(lane + k < PAGE_SIZE, pltpu.roll(y, PAGE_SIZE - k, 1), 0.0)
            k *= 2
        a = jnp.exp(ls + (y - u) + carry)
        acc = acc + jnp.sum(a[:, None, :] * vbuf[...], axis=2)
        return p - 1, carry + jnp.sum(u, axis=1, keepdims=True), acc

    st = lax.while_loop(cond, body, (jnp.int32(n_pages - 1), jnp.zeros((nh, 1), F32),
                                     jnp.zeros((nh, HEAD_DIM), F32)))
    o_ref[0] = st[2]


def sb_sample(q, cache_k, cache_v, layer, page_table):
    n, nh, hd = q.shape
    n_pages = page_table.shape[1]
    to_hds = lambda c: jnp.transpose(c, (0, 1, 3, 4, 2))
    grid_spec = pltpu.PrefetchScalarGridSpec(
        num_scalar_prefetch=1,
        grid=(n,),
        in_specs=[pl.BlockSpec((1, nh, hd, 1), lambda i, pt: (i, 0, 0, 0)),
                  pl.BlockSpec(memory_space=pl.ANY), pl.BlockSpec(memory_space=pl.ANY)],
        out_specs=pl.BlockSpec((1, nh, hd), lambda i, pt: (i, 0, 0)),
        scratch_shapes=[pltpu.VMEM((nh, hd, PAGE_SIZE), F32), pltpu.VMEM((nh, hd, PAGE_SIZE), F32),
                        pltpu.SemaphoreType.DMA((2,))],
    )
    return pl.pallas_call(
        functools.partial(_sb_sample_kernel, layer=layer, n_pages=n_pages),
        grid_spec=grid_spec,
        out_shape=jax.ShapeDtypeStruct((n, nh, hd), F32),
        compiler_params=_cparams(("arbitrary",)),
        name="sb_sample",
    )(page_table, q.reshape(n, nh, hd, 1), to_hds(cache_k), to_hds(cache_v))


def _paged_view(pool, layer, page_table):
    g = jnp.moveaxis(pool[layer, page_table], 2, -1)
    return g.reshape(g.shape[0], g.shape[1], -1, g.shape[-1])


def _dsa_select_kernel(qi_ref, wi_ref, kin_ref, kidx_ref, m_ref, mn_ref, scr, *, topk):
    npg, _, ps = kidx_ref.shape[1:]
    qi = qi_ref[0]
    wi = wi_ref[0]

    def score_body(p, c):
        s = jnp.dot(qi, kidx_ref[0, p], precision=HIGHEST, preferred_element_type=F32)
        scr[pl.ds(p, 1), :] = jnp.sum(wi * jnp.maximum(s, 0.0), axis=0, keepdims=True)
        return c

    lax.fori_loop(0, npg, score_body, 0, unroll=PAGES_PER_STEP)
    s_new = jnp.sum(wi * jnp.maximum(jnp.sum(qi * kin_ref[0], axis=1, keepdims=True), 0.0), axis=0, keepdims=True)
    keys = _sortable_key(scr[...])
    key_new = _sortable_key(s_new)
    idx = lax.broadcasted_iota(I32, keys.shape, 0) * ps + lax.broadcasted_iota(I32, keys.shape, 1)
    idx_new = npg * ps
    kf = float(topk)

    def total(hit, hit_new):
        t = jnp.sum(jnp.sum(hit.astype(F32), axis=1, keepdims=True), axis=0, keepdims=True)
        return t + hit_new.astype(F32)

    def bit_body(it, u):
        cu = u | lax.shift_left(jnp.int32(1), jnp.int32(31) - it)
        cand = cu ^ jnp.int32(INT_MIN)
        return jnp.where(total(keys >= cand, key_new >= cand) >= kf, cu, u)

    thr = lax.fori_loop(0, 32, bit_body, jnp.zeros((1, 1), I32)) ^ jnp.int32(INT_MIN)
    need = kf - total(keys > thr, key_new > thr)
    idx_bits = int(math.ceil(math.log2(idx_new + 1)))

    def j_body(it, j0):
        cj = j0 | lax.shift_left(jnp.int32(1), jnp.int32(idx_bits - 1) - it)
        f = total((keys == thr) & (idx < cj), (key_new == thr) & (idx_new < cj))
        return jnp.where(f < need, cj, j0)

    j0 = lax.fori_loop(0, idx_bits, j_body, jnp.zeros((1, 1), I32))
    m_ref[0] = ((keys > thr) | ((keys == thr) & (idx <= j0))).astype(F32)
    sel_new = ((key_new > thr) | ((key_new == thr) & (idx_new <= j0))).astype(F32)
    mn_ref[0] = jnp.broadcast_to(sel_new, (1, LANES))


def dsa_sample_select(qi, wi, ki_new, kidx_pages):
    n, nh, dd = qi.shape
    npg, ps = kidx_pages.shape[1], kidx_pages.shape[3]
    topk = min(IDX_TOPK, (npg * ps + 1) // 4)
    return pl.pallas_call(
        functools.partial(_dsa_select_kernel, topk=topk),
        grid=(n,),
        in_specs=[pl.BlockSpec((1, nh, dd), lambda i: (i, 0, 0)), pl.BlockSpec((1, nh, 1), lambda i: (i, 0, 0)),
                  pl.BlockSpec((1, 1, dd), lambda i: (i, 0, 0)),
                  pl.BlockSpec((1, npg, dd, ps), lambda i: (i, 0, 0, 0))],
        out_specs=[pl.BlockSpec((1, npg, ps), lambda i: (i, 0, 0)), pl.BlockSpec((1, 1, LANES), lambda i: (i, 0, 0))],
        out_shape=[jax.ShapeDtypeStruct((n, npg, ps), F32), jax.ShapeDtypeStruct((n, 1, LANES), F32)],
        scratch_shapes=[pltpu.VMEM((npg, ps), F32)],
        compiler_params=_cparams(("parallel",)),
        name="dsa_sample_select",
    )(qi, wi, ki_new, kidx_pages)


PAGES_PER_STEP = 8


def _paged_attn_kernel(q_ref, kn_ref, vn_ref, mn_ref, m0_ref, m1_ref, k_ref, v_ref, o_ref):
    nh = q_ref.shape[1]
    npg = k_ref.shape[1]
    qg = q_ref[0] * (1.0 / math.sqrt(HEAD_DIM))
    rowgrp = lax.broadcasted_iota(I32, (nh, 1), 0) // (nh // 2)
    assert npg % PAGES_PER_STEP == 0

    def body(t, carry):
        m, l, acc = carry
        p0 = t * PAGES_PER_STEP
        scores = []
        for i in range(PAGES_PER_STEP):
            mk = jnp.where(rowgrp == 0, m0_ref[0, pl.ds(p0 + i, 1), :], m1_ref[0, pl.ds(p0 + i, 1), :])
            scores.append(jnp.dot(qg, k_ref[0, p0 + i], precision=HIGHEST, preferred_element_type=F32) + _mask_bias(mk))
        m_new = m
        for s in scores:
            m_new = jnp.maximum(m_new, jnp.max(s, axis=1, keepdims=True))
        alpha = jnp.exp(m - m_new)
        l, acc = alpha * l, alpha * acc
        for i, s in enumerate(scores):
            pr = jnp.exp(s - m_new)
            l = l + jnp.sum(pr, axis=1, keepdims=True)
            acc = acc + lax.dot_general(pr, v_ref[0, p0 + i], (((1,), (1,)), ((), ())), precision=HIGHEST,
                                        preferred_element_type=F32)
        return m_new, l, acc

    m, l, acc = lax.fori_loop(0, npg // PAGES_PER_STEP, body, _flash_init(nh))
    s_n = jnp.sum(qg * kn_ref[0], axis=1, keepdims=True) + _mask_bias(mn_ref[0][:, 0:1])
    m_new = jnp.maximum(m, s_n)
    alpha = jnp.exp(m - m_new)
    p_n = jnp.exp(s_n - m_new)
    res = (alpha * acc + p_n * vn_ref[0]) / (alpha * l + p_n)
    o_ref[0] = jnp.where(rowgrp == 0, res, pltpu.roll(res, HEAD_DIM, 1))


def paged_masked_attention(q_placed, k_new, v_new, mask_new, masks, k_pages, v_pages):
    n, nh, _ = q_placed.shape
    row = lambda a: pl.BlockSpec((1,) + a.shape[1:], lambda i: (i,) + (0,) * (a.ndim - 1))
    ops = (q_placed, k_new, v_new, mask_new, masks[0], masks[1], k_pages, v_pages)
    return pl.pallas_call(
        _paged_attn_kernel,
        grid=(n,),
        in_specs=[row(a) for a in ops],
        out_specs=pl.BlockSpec((1, nh, LANES), lambda i: (i, 0, 0)),
        out_shape=jax.ShapeDtypeStruct((n, nh, LANES), F32),
        compiler_params=_cparams(("parallel",)),
        name="paged_masked_attention",
    )(*ops)


def _place_heads(q, groups):
    n, nh, dd = q.shape
    hpg = nh // groups
    z = jnp.zeros_like(q)
    return jnp.concatenate([jnp.concatenate([q[:, :hpg], z[:, :hpg]], axis=2),
                            jnp.concatenate([z[:, hpg:], q[:, hpg:]], axis=2)], axis=1)


def dsa_sample(p0s, cache_k, cache_v, cache_kidx, layer, page_table):
    n = p0s.shape[0]
    cut = lambda c, w: p0s[:, c:c + w]
    mask, mask_new = dsa_sample_select(cut(C_QI, 512).reshape(n, IDX_HEADS, IDX_DIM), cut(C_WG, IDX_HEADS).reshape(n, IDX_HEADS, 1),
                                       cut(C_KI, IDX_DIM).reshape(n, 1, IDX_DIM), _paged_view(cache_kidx, layer, page_table))
    per_head = lambda a: jnp.broadcast_to(a, (n, A_HEADS, LANES))
    o = paged_masked_attention(_place_heads(cut(C_QA, 512).reshape(n, A_HEADS, HEAD_DIM), A_KV_HEADS),
                               per_head(cut(C_KA, LANES).reshape(n, 1, LANES)), per_head(cut(C_VA, LANES).reshape(n, 1, LANES)),
                               per_head(mask_new), (mask, mask),
                               _paged_view(cache_k, layer, page_table), _paged_view(cache_v, layer, page_table))
    return o[:, :, :HEAD_DIM].reshape(n, A_HEADS * HEAD_DIM)


MOE_TM = 512
MOE_TF = 1792


def _router_kernel(x_ref, sc_ref, sh_ref, wr_ref, br_ref, h_ref, r_ref):
    h = x_ref[...] * (1.0 + sc_ref[0]) + sh_ref[0]
    h_ref[...] = h
    logits = jnp.dot(h, wr_ref[...], precision=HIGHEST, preferred_element_type=F32) + br_ref[...]
    lane = lax.broadcasted_iota(I32, logits.shape, 1)
    logits = jnp.where(lane < N_EXPERTS, logits, NEG)
    m1 = jnp.max(logits, axis=1, keepdims=True)
    lane_f = lane.astype(F32)
    i1 = jnp.min(jnp.where(logits == m1, lane_f, float(LANES)), axis=1, keepdims=True)
    rest = jnp.where(lane_f == i1, NEG, logits)
    m2 = jnp.max(rest, axis=1, keepdims=True)
    i2 = jnp.min(jnp.where(rest == m2, lane_f, float(LANES)), axis=1, keepdims=True)
    e2 = jnp.exp(m2 - m1)
    g1 = 1.0 / (1.0 + e2)
    g2 = e2 / (1.0 + e2)
    r_ref[...] = (jnp.where(lane == 0, i1, 0.0) + jnp.where(lane == 1, i2, 0.0)
                  + jnp.where(lane == 2, g1, 0.0) + jnp.where(lane == 3, g2, 0.0))


def router(x, sc, sh, w_router, b_router, seq, tm):
    m, d = x.shape
    tiles_per_seq = seq // tm
    wr = jnp.zeros((d, LANES), F32).at[:, :N_EXPERTS].set(w_router)
    br = jnp.zeros((1, LANES), F32).at[0, :N_EXPERTS].set(b_router)
    mod_spec = pl.BlockSpec((1, sc.shape[1], d), lambda i: (i // tiles_per_seq, 0, 0))
    return pl.pallas_call(
        _router_kernel,
        grid=(m // tm,),
        in_specs=[pl.BlockSpec((tm, d), lambda i: (i, 0)), mod_spec, mod_spec,
                  pl.BlockSpec((d, LANES), lambda i: (0, 0)), pl.BlockSpec((1, LANES), lambda i: (0, 0))],
        out_specs=[pl.BlockSpec((tm, d), lambda i: (i, 0)), pl.BlockSpec((tm, LANES), lambda i: (i, 0))],
        out_shape=[jax.ShapeDtypeStruct((m, d), F32), jax.ShapeDtypeStruct((m, LANES), F32)],
        compiler_params=_cparams(("parallel",)),
        name="router",
    )(x, sc, sh, wr, br)


def _gather_rows(idx_ref, src_hbm, dst, sem, n):
    def issue(r, c):
        pltpu.make_async_copy(src_hbm.at[pl.ds(idx_ref[r], 1)], dst.at[pl.ds(r, 1)], sem).start()
        return c
    lax.fori_loop(0, n, issue, 0)

    def drain(r, c):
        pltpu.make_async_copy(src_hbm.at[pl.ds(0, 1)], dst.at[pl.ds(0, 1)], sem).wait()
        return c
    lax.fori_loop(0, n, drain, 0)


def _moe_ffn_kernel(te_ref, nu_ref, tok_ref, h_hbm, wg_ref, wu_ref, wd_ref, o_ref, xbuf, xb16, acc, sem):
    i = pl.program_id(0)
    j = pl.program_id(1)
    last = pl.num_programs(1) - 1
    live = i < nu_ref[0]

    @pl.when(live)
    def _():
        @pl.when(j == 0)
        def _():
            _gather_rows(tok_ref, h_hbm, xbuf, sem, MOE_TM)
            xb16[...] = xbuf[...].astype(BF16)
            acc[...] = jnp.zeros_like(acc)

        x = xb16[...]
        a = (_silu(_dot(x, wg_ref[0])) * _dot(x, wu_ref[0])).astype(BF16)
        acc[...] += _dot(a, wd_ref[0])

        @pl.when(j == last)
        def _():
            o_ref[...] = acc[...]

    @pl.when(jnp.logical_not(live) & (j == last))
    def _():
        o_ref[...] = jnp.zeros_like(o_ref)


def moe_expert_rows(h, tok_of_slot, tile_expert, n_used, wg, wu, wd):
    d = h.shape[1]
    s_pad = tok_of_slot.shape[0]
    n_tiles = s_pad // MOE_TM
    n_ff = D_FF_EXPERT // MOE_TF
    grid_spec = pltpu.PrefetchScalarGridSpec(
        num_scalar_prefetch=2,
        grid=(n_tiles, n_ff),
        in_specs=[pl.BlockSpec((MOE_TM,), lambda i, j, te, nu: (i,), memory_space=pltpu.SMEM),
                  pl.BlockSpec(memory_space=pl.ANY),
                  pl.BlockSpec((1, d, MOE_TF), lambda i, j, te, nu: (te[i], 0, j)),
                  pl.BlockSpec((1, d, MOE_TF), lambda i, j, te, nu: (te[i], 0, j)),
                  pl.BlockSpec((1, MOE_TF, d), lambda i, j, te, nu: (te[i], j, 0))],
        out_specs=pl.BlockSpec((MOE_TM, d), lambda i, j, te, nu: (i, 0)),
        scratch_shapes=[pltpu.VMEM((MOE_TM, d), F32), pltpu.VMEM((MOE_TM, d), BF16),
                        pltpu.VMEM((MOE_TM, d), F32), pltpu.SemaphoreType.DMA(())],
    )
    return pl.pallas_call(
        _moe_ffn_kernel,
        grid_spec=grid_spec,
        out_shape=jax.ShapeDtypeStruct((s_pad, d), F32),
        compiler_params=_cparams(("arbitrary", "arbitrary")),
        name="moe_ffn",
    )(tile_expert, n_used, tok_of_slot, h, wg, wu, wd)


MOE_TC = 256


def _moe_combine_kernel(s0_ref, s1_ref, ys_hbm, r_ref, x_ref, gt_ref, g_ref, b_ref, o_ref, y0, y1, sem, *, tc):
    _gather_rows(s0_ref, ys_hbm, y0, sem, tc)
    _gather_rows(s1_ref, ys_hbm, y1, sem, tc)
    r = r_ref[...]
    f = r[:, 2:3] * y0[...] + r[:, 3:4] * y1[...]
    y = ALPHA * x_ref[...] + gt_ref[0] * f
    o_ref[...] = _layer_norm_rows(y, g_ref[...], b_ref[...])


def moe_combine_ln(ys, slot0, slot1, r, x, gt, ln_g, ln_b, seq, tc):
    m, d = x.shape
    tiles_per_seq = seq // tc
    return pl.pallas_call(
        functools.partial(_moe_combine_kernel, tc=tc),
        grid=(m // tc,),
        in_specs=[pl.BlockSpec((tc,), lambda i: (i,), memory_space=pltpu.SMEM),
                  pl.BlockSpec((tc,), lambda i: (i,), memory_space=pltpu.SMEM),
                  pl.BlockSpec(memory_space=pl.ANY),
                  pl.BlockSpec((tc, LANES), lambda i: (i, 0)),
                  pl.BlockSpec((tc, d), lambda i: (i, 0)),
                  pl.BlockSpec((1, gt.shape[1], d), lambda i: (i // tiles_per_seq, 0, 0)),
                  pl.BlockSpec((1, d), lambda i: (0, 0)), pl.BlockSpec((1, d), lambda i: (0, 0))],
        out_specs=pl.BlockSpec((tc, d), lambda i: (i, 0)),
        out_shape=jax.ShapeDtypeStruct((m, d), F32),
        scratch_shapes=[pltpu.VMEM((tc, d), F32), pltpu.VMEM((tc, d), F32), pltpu.SemaphoreType.DMA(())],
        compiler_params=_cparams(("arbitrary",)),
        name="moe_combine_ln",
    )(slot0, slot1, ys, r, x, gt, ln_g.reshape(1, d), ln_b.reshape(1, d))


def moe_routing_tables(r):
    t = r.shape[0]
    experts = r[:, :TOP_K].astype(I32).reshape(-1)
    onehot = (experts[:, None] == jnp.arange(N_EXPERTS, dtype=I32)[None, :]).astype(I32)
    rank = jnp.sum((jnp.cumsum(onehot, axis=0) - onehot) * onehot, axis=1)
    counts = jnp.sum(onehot, axis=0)
    padded = ((counts + MOE_TM - 1) // MOE_TM) * MOE_TM
    ends = jnp.cumsum(padded)
    starts = ends - padded
    slot = starts[experts] + rank
    n_tiles = -(-(TOP_K * t) // MOE_TM) + N_EXPERTS
    s_pad = n_tiles * MOE_TM
    tok_of_slot = jnp.zeros((s_pad,), I32).at[slot].set(jnp.arange(TOP_K * t, dtype=I32) // TOP_K)
    tile_start = jnp.arange(n_tiles, dtype=I32) * MOE_TM
    tile_expert = jnp.minimum(jnp.sum((tile_start[:, None] >= ends[None, :]).astype(I32), axis=1), N_EXPERTS - 1)
    n_used = (ends[-1] // MOE_TM).astype(I32).reshape(1)
    slot2 = slot.reshape(t, TOP_K)
    return tok_of_slot, tile_expert.astype(I32), n_used, slot2[:, 0], slot2[:, 1]


def rope_tables(pos):
    half = ROPE_DIM // 2
    inv = ROPE_THETA ** (-jnp.arange(half, dtype=F32) / half)
    ang = pos.astype(F32)[:, None] * inv[None, :]
    cos, sin = jnp.cos(ang), jnp.sin(ang)
    ones = jnp.ones((pos.shape[0], HEAD_DIM - ROPE_DIM), F32)
    zeros = jnp.zeros_like(ones)
    zh = jnp.zeros_like(sin)
    c64 = jnp.concatenate([cos, cos, ones], axis=1)
    sn64 = jnp.concatenate([-sin, zh, zeros], axis=1)
    sp64 = jnp.concatenate([zh, sin, zeros], axis=1)
    return tuple(jnp.concatenate([t, t], axis=1) for t in (c64, sn64, sp64))


def _even_weight_layout(w_in):
    o = (0,) + EVEN_OFFSETS
    kv = o[7]
    sec = lambda start, width: np.arange(start, start + width)
    pad = lambda width: np.full((width,), -1)
    perm = np.concatenate([
        sec(o[0], 512), sec(o[3], 512), sec(o[6], 512),
        sec(o[1], 128), sec(kv + 2 * 128, 128), sec(kv + 4 * 128, 128), sec(o[4], 64), pad(64),
        sec(o[6], 512), sec(o[2], 128), sec(kv, 128), sec(kv + 128, 128),
        sec(kv + 3 * 128, 128), sec(kv + 5 * 128, 128), sec(o[5], 8), sec(o[8], 24), pad(96)])
    assert perm.shape[0] == EV_W
    w_ext = jnp.concatenate([w_in, jnp.zeros((w_in.shape[0], 1), w_in.dtype)], axis=1)
    return w_ext[:, np.where(perm < 0, w_in.shape[1], perm)]


def _masked_softmax(s, mask):
    s = jnp.where(mask, s, NEG)
    return jax.nn.softmax(s, axis=-1) * mask


def _shared_attend(q, k, v, mask):
    n, tq, h, d = q.shape
    g = k.shape[2]
    qg = q.reshape(n, tq, g, h // g, d)
    s = jnp.einsum('ntgrd,nsgd->ntgrs', qg, k, precision=HIGHEST) / math.sqrt(d)
    p = _masked_softmax(s, mask[None, :, None, None, :])
    o = jnp.einsum('ntgrs,nsgd->ntgrd', p, v, precision=HIGHEST)
    return o.reshape(n, tq, h, d), p


def _gathered_attend(q, k, v, valid):
    n, tq, h, d = q.shape
    g = k.shape[2]
    qg = q.reshape(n, tq, g, h // g, d)
    s = jnp.einsum('ntgrd,ntgsd->ntgrs', qg, k, precision=HIGHEST) / math.sqrt(d)
    p = _masked_softmax(s, valid[:, :, :, None, :])
    o = jnp.einsum('ntgrs,ntgsd->ntgrd', p, v, precision=HIGHEST)
    return o.reshape(n, tq, h, d)


def _gather_pages(pool, layer, page_table):
    g = pool[layer, page_table]
    return g.reshape((g.shape[0], g.shape[1] * g.shape[2]) + g.shape[3:])


def _to_blocks(rows):
    n, l, g, d = rows.shape
    n_slc = -(-l // SLC_LEN)
    rows = jnp.pad(rows, ((0, 0), (0, n_slc * SLC_LEN - l), (0, 0), (0, 0)))
    return rows.reshape(n, n_slc, SLC_LEN, g, d)


def _cmp_to_slc(n_cmp, n_slc):
    i = np.arange(n_cmp)[:, None] * CMP_STRIDE
    j = np.arange(n_slc)[None, :] * SLC_LEN
    return jnp.asarray((i < j + SLC_LEN) & (i + CMP_LEN > j), dtype=F32)


def _nsa_attend_sample(q_rot, q_raw, gates, q_pos, kc, vc, ks_blk, vs_blk, kw, vw, kw_pos):
    n, tq, h, d = q_rot.shape
    g = kc.shape[2]
    n_cmp, n_slc = kc.shape[1], ks_blk.shape[1]
    cmp_end = jnp.arange(n_cmp) * CMP_STRIDE + (CMP_LEN - 1)
    o_cmp, p_cmp = _shared_attend(q_raw, kc, vc, cmp_end[None, :] <= q_pos[:, None])
    p_slc = jnp.einsum('ntgrc,cj->ntgj', p_cmp, _cmp_to_slc(n_cmp, n_slc), precision=HIGHEST)
    blk = jnp.arange(n_slc)[None, :]
    cur = (q_pos // SLC_LEN)[:, None]
    forced = (blk == 0) | (blk == cur) | (blk == cur - 1)
    avail = blk * SLC_LEN <= q_pos[:, None]
    score = jnp.where(avail[None, :, None, :], p_slc + FORCE_BONUS * forced[None, :, None, :], NEG)
    _, sel = lax.top_k(score, min(SLC_TOPN, n_slc))
    bn = jnp.arange(n)[:, None, None, None]
    gi = jnp.arange(g)[None, None, :, None]
    ks = ks_blk[bn, sel, :, gi].reshape(n, tq, g, -1, d)
    vs = vs_blk[bn, sel, :, gi].reshape(n, tq, g, -1, d)
    sel_pos = (sel[..., None] * SLC_LEN + jnp.arange(SLC_LEN)).reshape(n, tq, g, -1)
    o_slc = _gathered_attend(q_rot, ks, vs, sel_pos <= q_pos[None, :, None, None])
    wmask = ((kw_pos[None, :] <= q_pos[:, None]) & (kw_pos[None, :] > q_pos[:, None] - WINDOW)
             & (kw_pos[None, :] >= 0))
    o_win, _ = _shared_attend(q_rot, kw, vw, wmask)
    gt = jax.nn.sigmoid(gates)
    return gt[..., 0:1] * o_cmp + gt[..., 1:2] * o_slc + gt[..., 2:3] * o_win


def _nsa_sample(q_rot, qb, gb, kc, vc, ks, vs, kw, vw, pos, cache_ck, cache_cv, cache_sk, cache_sv,
                win_k, win_v, layer, page_table, cmp_k, cmp_v):
    cat = lambda pool, new: jnp.concatenate([_gather_pages(pool, layer, page_table), new], axis=1)
    n, tq, g, dh = kc.shape
    past = page_table.shape[1] * PAGE_SIZE
    n_cmp = (past + tq - CMP_LEN) // CMP_STRIDE + 1
    assert (n_cmp + CMP_LEN // CMP_STRIDE - 1) * CMP_STRIDE <= past

    def compress_past(pool, cmp_w):
        return compress_pages(_paged_view(pool, layer, page_table), *cmp_w)[:, :n_cmp].reshape(n, n_cmp, g, dh)

    kcc = compress_past(cache_ck, cmp_k)
    vcc = compress_past(cache_cv, cmp_v)
    ks_blk, vs_blk = _to_blocks(cat(cache_sk, ks)), _to_blocks(cat(cache_sv, vs))
    w, tq = win_k.shape[1], qb.shape[1]
    kw_all = jnp.concatenate([win_k, kw], axis=1)
    vw_all = jnp.concatenate([win_v, vw], axis=1)
    kw_pos = pos[0] - w + jnp.arange(w + tq)
    o = _nsa_attend_sample(q_rot, qb, gb, pos, kcc, vcc, ks_blk, vs_blk, kw_all, vw_all, kw_pos)
    return o, kw_all[:, -w:], vw_all[:, -w:]


def kernel(x_prompt, x_sample, cache_a_k, cache_a_v, cache_a_kidx, cache_b_cmp_k, cache_b_cmp_v, cache_b_slc_k, cache_b_slc_v, state_b_win_k, state_b_win_v, cache_c_k, cache_c_v, page_table, c_prompt, c_sample, w_ada_mix, b_ada_mix, ln_mix_g, ln_mix_b, w_ada_ffn, b_ada_ffn, ln_ffn_g, ln_ffn_b, w_in_even, w_out_even, cmp_pos_k, cmp_w1_k, cmp_w2_k, cmp_pos_v, cmp_w1_v, cmp_w2_v, w_ffn_gate, w_ffn_up, w_ffn_down, w_in_odd, w_out_odd, w_router, b_router, w_moe_gate, w_moe_up, w_moe_down):
    n_p, t_p, d = x_prompt.shape
    n_s, t_s = x_sample.shape[:2]
    past = page_table.shape[1] * PAGE_SIZE
    pos_p = jnp.arange(t_p)
    pos_s = past + jnp.arange(t_s)
    m_p = n_p * t_p
    tm = 512

    n_c = n_p + n_s
    mp = -(-n_c // SUBLANES) * SUBLANES
    c_all = jnp.zeros((mp, d), F32).at[:n_p].set(c_prompt).at[n_p:n_c].set(c_sample)
    ada_mix = ada_all(c_all, w_ada_mix, b_ada_mix)
    ada_ffn = ada_all(c_all, w_ada_ffn, b_ada_ffn)

    assert t_s == 1, "the decode kernels handle one new token per sequence"
    m_s = n_s * t_s

    def mods(ada, layer):
        sh, sc, gt = jnp.split(ada[layer], 3, axis=-1)
        pr = tuple(a[:n_p].reshape(n_p, 1, d) for a in (sh, sc, gt))
        sa = tuple(jnp.repeat(a[n_p:n_c], t_s, axis=0).reshape(1, m_s, d) for a in (sh, sc, gt))
        return pr, sa

    xp = x_prompt.reshape(m_p, d)
    xs = x_sample.reshape(m_s, d)
    tabs = rope_tables(pos_p)
    tabs_s = rope_tables(jnp.tile(pos_s, n_s))

    (sh_p, sc_p, gt_p), (sh_s, sc_s, gt_s) = mods(ada_mix, 0)
    w_even_f32 = _even_weight_layout(w_in_even[0])
    w_even = w_even_f32.astype(BF16)
    p0 = mod_project(xp, sc_p, sh_p, w_even, tabs, EV_ROPE_W, t_p, tm, 256)
    col = lambda c, w: p0[:, c:c + w]
    kc_p, vc_p = col(C_KC, 128).reshape(n_p, t_p, 128), col(C_VC, 128).reshape(n_p, t_p, 128)
    kcc = compress_rows(kc_p, cmp_pos_k[0], cmp_w1_k[0], cmp_w2_k[0])
    vcc = compress_rows(vc_p, cmp_pos_v[0], cmp_w1_v[0], cmp_w2_v[0])
    oa = dsa_prompt(p0, n_p, t_p)
    ob = nsa_prompt(p0, kcc, vcc, n_p, t_p)
    w_out0 = w_out_even[0].astype(BF16)
    xp = outproj_ln(oa, 0, ob, 0, w_out0, xp, gt_p, ln_mix_g[0], ln_mix_b[0], t_p, tm)

    kv4 = lambda a: a.reshape(1, n_p, t_p, 2, HEAD_DIM)
    wlen = min(WINDOW, t_p)
    even_p = dict(
        a_k=kv4(col(C_KA, 128)), a_v=kv4(col(C_VA, 128)), a_kidx=col(C_KI, 64).reshape(1, n_p, t_p, IDX_DIM),
        cmp_k=kv4(kc_p), cmp_v=kv4(vc_p), slc_k=kv4(col(C_KS, 128)), slc_v=kv4(col(C_VS, 128)),
        win_k=kv4(col(C_KW, 128))[:, :, -wlen:], win_v=kv4(col(C_VW, 128))[:, :, -wlen:])

    p0s = mod_project(xs, sc_s, sh_s, w_even_f32, tabs_s, EV_ROPE_W, m_s, m_s, 256)
    cs = lambda c, w, *shape: p0s[:, c:c + w].reshape((n_s, t_s) + shape)
    kvs = lambda c: cs(c, 128, 2, HEAD_DIM)
    ka, va, ki = kvs(C_KA), kvs(C_VA), cs(C_KI, IDX_DIM, IDX_DIM)
    kc, vc, ks, vs, kw, vw = kvs(C_KC), kvs(C_VC), kvs(C_KS), kvs(C_VS), kvs(C_KW), kvs(C_VW)
    oa_s = dsa_sample(p0s, cache_a_k, cache_a_v, cache_a_kidx, 0, page_table)
    cmp_k = (cmp_pos_k[0], cmp_w1_k[0], cmp_w2_k[0])
    cmp_v = (cmp_pos_v[0], cmp_w1_v[0], cmp_w2_v[0])
    ob_s, wk, wv = _nsa_sample(cs(C_QBR, 512, B_HEADS, HEAD_DIM), cs(C_QB, 512, B_HEADS, HEAD_DIM),
                               cs(C_WG + WG_GB_LANE, 3 * B_HEADS, B_HEADS, 3), kc, vc, ks, vs, kw, vw, pos_s,
                               cache_b_cmp_k, cache_b_cmp_v, cache_b_slc_k, cache_b_slc_v,
                               state_b_win_k[0], state_b_win_v[0], 0, page_table, cmp_k, cmp_v)
    xs = outproj_ln(oa_s.reshape(m_s, -1), 0, ob_s.reshape(m_s, -1), 0, w_out_even[0], xs, gt_s,
                    ln_mix_g[0], ln_mix_b[0], m_s, m_s)
    even_s = (ka, va, ki, kc, vc, ks, vs, wk, wv)

    (sh_p, sc_p, gt_p), (sh_s, sc_s, gt_s) = mods(ada_ffn, 0)
    ffn_w = (w_ffn_gate[0].astype(BF16), w_ffn_up[0].astype(BF16), w_ffn_down[0].astype(BF16))
    xp = ffn_ln(xp, sc_p, sh_p, gt_p, *ffn_w, ln_ffn_g[0], ln_ffn_b[0], t_p, tm, D_FF // 2)
    xs = ffn_ln(xs, sc_s, sh_s, gt_s, w_ffn_gate[0], w_ffn_up[0], w_ffn_down[0], ln_ffn_g[0], ln_ffn_b[0],
                m_s, m_s, D_FF // 2)

    (sh_p, sc_p, gt_p), (sh_s, sc_s, gt_s) = mods(ada_mix, 1)
    no_rope = lambda rows: (jnp.ones((rows, LANES), F32), jnp.zeros((rows, LANES), F32), jnp.zeros((rows, LANES), F32))
    w_odd = w_in_odd[0].astype(BF16)
    w_out1 = w_out_odd[0].astype(BF16)
    p1 = mod_project(xp, sc_p, sh_p, w_odd, no_rope(t_p), 0, t_p, tm, 512)
    osb = sb_prompt(p1, n_p, t_p)
    xp = outproj_ln(osb, 0, osb, 1, w_out1, xp, gt_p, ln_mix_g[1], ln_mix_b[1], t_p, tm)
    hd = C_HEADS * HEAD_DIM
    c_k_p = p1[:, hd:2 * hd].reshape(1, n_p, t_p, C_HEADS, HEAD_DIM)
    c_v_p = p1[:, 2 * hd:].reshape(1, n_p, t_p, C_HEADS, HEAD_DIM)

    p1s = mod_project(xs, sc_s, sh_s, w_in_odd[0], no_rope(m_s), 0, m_s, m_s, 512)
    shp = (n_s, t_s, C_HEADS, HEAD_DIM)
    k_s, v_s = p1s[:, hd:2 * hd].reshape(shp), p1s[:, 2 * hd:].reshape(shp)
    osb_s = sb_sample(p1s[:, :hd].reshape(m_s, C_HEADS, HEAD_DIM), cache_c_k, cache_c_v, 0, page_table).reshape(m_s, hd)
    xs = outproj_ln(osb_s, 0, osb_s, 1, w_out_odd[0], xs, gt_s, ln_mix_g[1], ln_mix_b[1], m_s, m_s)

    (sh_p, sc_p, gt_p), (sh_s, sc_s, gt_s) = mods(ada_ffn, 1)
    h_p, r_p = router(xp, sc_p, sh_p, w_router[0], b_router[0], t_p, tm)
    h_s, r_s = router(xs, sc_s, sh_s, w_router[0], b_router[0], m_s, m_s)
    tok_of_slot, tile_expert, n_used, slot0, slot1 = moe_routing_tables(jnp.concatenate([r_p, r_s], axis=0))
    ys = moe_expert_rows(jnp.concatenate([h_p, h_s], axis=0), tok_of_slot, tile_expert, n_used,
                         w_moe_gate[0].astype(BF16), w_moe_up[0].astype(BF16), w_moe_down[0].astype(BF16))
    xp = moe_combine_ln(ys, slot0[:m_p], slot1[:m_p], r_p, xp, gt_p, ln_ffn_g[1], ln_ffn_b[1], t_p, MOE_TC)
    xs = moe_combine_ln(ys, slot0[m_p:], slot1[m_p:], r_s, xs, gt_s, ln_ffn_g[1], ln_ffn_b[1], m_s, m_s)

    st = lambda a: a[None]
    (a_k_s, a_v_s, a_kidx_s, b_cmp_k_s, b_cmp_v_s, b_slc_k_s, b_slc_v_s, b_win_k_s, b_win_v_s) = [st(a) for a in even_s]
    return (xp.reshape(n_p, t_p, d), xs.reshape(n_s, t_s, d),
            even_p['a_k'], a_k_s, even_p['a_v'], a_v_s, even_p['a_kidx'], a_kidx_s,
            even_p['cmp_k'], b_cmp_k_s, even_p['cmp_v'], b_cmp_v_s,
            even_p['slc_k'], b_slc_k_s, even_p['slc_v'], b_slc_v_s,
            even_p['win_k'], b_win_k_s, even_p['win_v'], b_win_v_s,
            c_k_p, st(k_s), c_v_p, st(v_s))
```
